```python
import math
import jax
import jax.numpy as jnp
from jax import lax
import numpy as np


D_MODEL = 1024
BATCH = 16
SEQ = 2048
DEPTH = 2
DEC_BATCH = 32
DEC_SEQ = 8
PAST_LEN = 16384
PAGE_SIZE = 128

POOL_GROUPS = 4
POOL_WINDOWS = (2, 4, 8, 16)
POOL_W = D_MODEL
POOL_GW = POOL_W // POOL_GROUPS
POOL_BUF = max(POOL_WINDOWS) - 1
RET_HEADS = 8
RET_DK = D_MODEL // RET_HEADS
RET_DV = 2 * RET_DK
RET_QK_W = RET_HEADS * RET_DK
RET_V_W = RET_HEADS * RET_DV
RET_CHUNK = 128
RET_ROPE_BASE = 10000.0
SB_HEADS = 16
SB_DH = D_MODEL // SB_HEADS
SB_W = SB_HEADS * SB_DH
SB_BLOCK = 128
SB_BIAS_INIT = -6.0
IN_WIDTHS = (POOL_W, POOL_W, RET_QK_W, RET_QK_W, RET_V_W, RET_V_W, SB_W, SB_W, SB_W, SB_W, D_MODEL, D_MODEL, D_MODEL)
IN_W = sum(IN_WIDTHS)
DEEPNORM_ALPHA = (2 * DEPTH) ** 0.25
DEEPNORM_BETA = (8 * DEPTH) ** -0.25
LN_EPS = 1e-5
GN_EPS = 1e-6

kernel_name = "hybrid_pool_retention_stickbreak_step"


def layer_norm(x, g, b):
    xf = x.astype(jnp.float32)
    mu = xf.mean(-1, keepdims=True)
    var = jnp.square(xf - mu).mean(-1, keepdims=True)
    return ((xf - mu) * lax.rsqrt(var + LN_EPS) * g + b).astype(x.dtype)


def pool_mixer(u, buf, pos, pool_w, pool_scale):
    B, T, _ = u.shape
    P = POOL_BUF
    u_ext = jnp.concatenate([buf.astype(u.dtype), u], axis=1)
    cs = jnp.cumsum(u_ext.astype(jnp.float32), axis=1)
    cs = jnp.concatenate([jnp.zeros_like(cs[:, :1]), cs], axis=1)
    means = []
    for g, w in enumerate(POOL_WINDOWS):
        sl = slice(g * POOL_GW, (g + 1) * POOL_GW)
        s = cs[:, P + 1:P + T + 1, sl] - cs[:, P + 1 - w:P + T + 1 - w, sl]
        cnt = jnp.minimum(pos + 1.0, float(w))[None, :, None]
        means.append(s / cnt)
    mean = jnp.concatenate(means, axis=-1)
    diff = (mean - u.astype(jnp.float32)).astype(u.dtype).reshape(B, T, POOL_GROUPS, POOL_GW)
    mixed = jnp.einsum('btgc,gcd->btgd', diff, pool_w).reshape(B, T, POOL_W)
    return mixed * pool_scale, u_ext[:, -P:]


def rotary(x, pos):
    half = x.shape[-1] // 2
    freqs = RET_ROPE_BASE ** (-jnp.arange(half, dtype=jnp.float32) / half)
    ang = pos[:, None] * freqs[None, :]
    cos = jnp.cos(ang)[None, :, None, :]
    sin = jnp.sin(ang)[None, :, None, :]
    x1, x2 = x[..., :half], x[..., half:]
    return jnp.concatenate([x1 * cos - x2 * sin, x1 * sin + x2 * cos], axis=-1)


def retention_chunk(S, q, k, v, log_g):
    C = q.shape[1]
    idx = jnp.arange(C, dtype=jnp.float32)
    dist = idx[:, None] - idx[None, :]
    D = jnp.where(dist >= 0, jnp.exp(log_g[:, None, None] * jnp.maximum(dist, 0.0)), 0.0)
    scores = jnp.einsum('bihd,bjhd->bhij', q, k) * D[None]
    o = jnp.einsum('bhij,bjhv->bihv', scores, v)
    q_dec = jnp.exp(log_g[None, :] * (idx[:, None] + 1.0))
    o = o + jnp.einsum('bihd,bhdv->bihv', q, S) * q_dec[None, :, :, None]
    k_dec = jnp.exp(log_g[None, :] * (C - 1.0 - idx[:, None]))
    S_new = jnp.exp(log_g * C)[None, :, None, None] * S + jnp.einsum('bjhd,bjhv->bhdv', k * k_dec[None, :, :, None], v)
    return S_new, o


def retention(q, k, v, S0, pos):
    B, T, _ = q.shape
    log_g = jnp.log1p(-(2.0 ** (-5.0 - jnp.arange(RET_HEADS, dtype=jnp.float32))))
    q = rotary(q.reshape(B, T, RET_HEADS, RET_DK).astype(jnp.float32), pos)
    k = rotary(k.reshape(B, T, RET_HEADS, RET_DK).astype(jnp.float32), pos) * (RET_DK ** -0.5)
    v = v.reshape(B, T, RET_HEADS, RET_DV).astype(jnp.float32)
    C = RET_CHUNK if T % RET_CHUNK == 0 else T
    N = T // C

    def chunks(t):
        return jnp.moveaxis(t.reshape(B, N, C, *t.shape[2:]), 1, 0)

    S, o = lax.scan(lambda s, qkv: retention_chunk(s, qkv[0], qkv[1], qkv[2], log_g),
                    S0.astype(jnp.float32), (chunks(q), chunks(k), chunks(v)))
    o = jnp.moveaxis(o, 0, 1).reshape(B, T, RET_HEADS, RET_DV)
    mu = o.mean(-1, keepdims=True)
    var = jnp.square(o - mu).mean(-1, keepdims=True)
    o = (o - mu) * lax.rsqrt(var + GN_EPS)
    return o.reshape(B, T, RET_V_W), S


def stick_breaking(q, k_all, v_all, offset, sb_bias):
    T = q.shape[1]
    bias = sb_bias.astype(jnp.float32)[None, :, None, None]
    outs = []
    for start in range(0, T, SB_BLOCK):
        stop = min(start + SB_BLOCK, T)
        n_keys = offset + stop
        kb = k_all[:, :n_keys]
        vb = v_all[:, :n_keys]
        z = jnp.einsum('bqhd,bkhd->bhqk', q[:, start:stop], kb,
                       preferred_element_type=jnp.float32) * (SB_DH ** -0.5) + bias
        q_pos = offset + start + jnp.arange(stop - start)
        mask = jnp.arange(n_keys)[None, :] < q_pos[:, None]
        log_keep = jnp.where(mask, jax.nn.log_sigmoid(-z), 0.0)
        suffix = lax.cumsum(log_keep, axis=3, reverse=True) - log_keep
        A = jnp.where(mask, jnp.exp(jax.nn.log_sigmoid(z) + suffix), 0.0)
        outs.append(jnp.einsum('bhqk,bkhd->bqhd', A.astype(vb.dtype), vb))
    return jnp.concatenate(outs, axis=1)


def mixer_layer(x, pos, pool_buf, ret_state, k_past, v_past,
                w_in, pool_w, pool_scale, proj_pool, proj_ret, proj_sb, w_out, ln_g, ln_b, sb_bias):
    B, T, _ = x.shape
    split_at = np.cumsum(IN_WIDTHS)[:-1].tolist()
    (u, g_pool, rq, rk, rv, g_ret, sq, sk, sv, g_sb,
     m_pool, m_ret, m_sb) = jnp.split(x @ w_in, split_at, axis=-1)
    a, pool_new = pool_mixer(u, pool_buf, pos, pool_w, pool_scale)
    a = a * jax.nn.silu(g_pool)
    b, ret_new = retention(rq, rk, rv, ret_state, pos)
    b = b.astype(x.dtype) * jax.nn.silu(g_ret)
    q = sq.reshape(B, T, SB_HEADS, SB_DH)
    k = sk.reshape(B, T, SB_HEADS, SB_DH)
    v = sv.reshape(B, T, SB_HEADS, SB_DH)
    if k_past is None:
        k_all, v_all, offset = k, v, 0
    else:
        k_all = jnp.concatenate([k_past.astype(k.dtype), k], axis=1)
        v_all = jnp.concatenate([v_past.astype(v.dtype), v], axis=1)
        offset = k_past.shape[1]
    c = stick_breaking(q, k_all, v_all, offset, sb_bias).reshape(B, T, SB_W).astype(x.dtype) * jax.nn.silu(g_sb)
    merged = (jax.nn.sigmoid(m_pool) * (a @ proj_pool)
              + jax.nn.sigmoid(m_ret) * (b @ proj_ret)
              + jax.nn.sigmoid(m_sb) * (c @ proj_sb))
    y = layer_norm(DEEPNORM_ALPHA * x + merged @ w_out, ln_g, ln_b)
    return y, pool_new, ret_new.astype(x.dtype), k, v


def setup_inputs(seed: int = 0) -> dict:
    key = jax.random.key(seed)
    ks = jax.random.split(key, 20)
    n_pages = PAST_LEN // PAGE_SIZE
    n_used = DEC_BATCH * n_pages
    n_phys = n_used + n_used // 4
    nrm = jax.random.normal
    x_prompt = nrm(ks[0], (BATCH, SEQ, D_MODEL), jnp.float32)
    x_sample = nrm(ks[1], (DEC_BATCH, DEC_SEQ, D_MODEL), jnp.float32)
    state_pool = nrm(ks[2], (DEPTH, DEC_BATCH, POOL_BUF, POOL_W), jnp.float32)
    state_ret = 0.5 * nrm(ks[3], (DEPTH, DEC_BATCH, RET_HEADS, RET_DK, RET_DV), jnp.float32)
    cache_sb_k = nrm(ks[4], (DEPTH, n_phys, PAGE_SIZE, SB_HEADS, SB_DH), jnp.float32)
    cache_sb_v = nrm(ks[5], (DEPTH, n_phys, PAGE_SIZE, SB_HEADS, SB_DH), jnp.float32)
    page_table = jax.random.permutation(ks[6], n_phys)[:n_used].reshape(DEC_BATCH, n_pages).astype(jnp.int32)
    ln_in_g = 1.0 + 0.05 * nrm(ks[7], (D_MODEL,), jnp.float32)
    ln_in_b = 0.02 * nrm(ks[8], (D_MODEL,), jnp.float32)
    w_in = nrm(ks[9], (DEPTH, D_MODEL, IN_W), jnp.float32) * D_MODEL ** -0.5
    pool_w = nrm(ks[10], (DEPTH, POOL_GROUPS, POOL_GW, POOL_GW), jnp.float32) * POOL_GW ** -0.5
    pool_scale = 1.0 + 0.1 * nrm(ks[11], (DEPTH, POOL_W), jnp.float32)
    proj_pool = nrm(ks[12], (DEPTH, POOL_W, D_MODEL), jnp.float32) * (POOL_W ** -0.5 * DEEPNORM_BETA)
    proj_ret = nrm(ks[13], (DEPTH, RET_V_W, D_MODEL), jnp.float32) * (RET_V_W ** -0.5 * DEEPNORM_BETA)
    proj_sb = nrm(ks[14], (DEPTH, SB_W, D_MODEL), jnp.float32) * (SB_W ** -0.5 * DEEPNORM_BETA)
    w_out = nrm(ks[15], (DEPTH, D_MODEL, D_MODEL), jnp.float32) * (D_MODEL ** -0.5 * DEEPNORM_BETA)
    ln_g = 1.0 + 0.05 * nrm(ks[16], (DEPTH, D_MODEL), jnp.float32)
    ln_b = 0.02 * nrm(ks[17], (DEPTH, D_MODEL), jnp.float32)
    sb_bias = SB_BIAS_INIT + 0.1 * nrm(ks[18], (DEPTH, SB_HEADS), jnp.float32)
    return {"x_prompt": x_prompt, "x_sample": x_sample, "state_pool": state_pool,
            "state_ret": state_ret, "cache_sb_k": cache_sb_k, "cache_sb_v": cache_sb_v,
            "page_table": page_table, "ln_in_g": ln_in_g, "ln_in_b": ln_in_b, "w_in": w_in,
            "pool_w": pool_w, "pool_scale": pool_scale, "proj_pool": proj_pool,
            "proj_ret": proj_ret, "proj_sb": proj_sb, "w_out": w_out, "ln_g": ln_g, "ln_b": ln_b,
            "sb_bias": sb_bias}


def reference(x_prompt, x_sample, state_pool, state_ret, cache_sb_k, cache_sb_v, page_table,
              ln_in_g, ln_in_b, w_in, pool_w, pool_scale, proj_pool, proj_ret, proj_sb,
              w_out, ln_g, ln_b, sb_bias):
    Bp, Tp, _ = x_prompt.shape
    Bs, Ts, _ = x_sample.shape
    past_len = page_table.shape[1] * cache_sb_k.shape[2]
    pos_p = jnp.arange(Tp, dtype=jnp.float32)
    pos_s = past_len + jnp.arange(Ts, dtype=jnp.float32)
    xp = layer_norm(x_prompt, ln_in_g, ln_in_b)
    xs = layer_norm(x_sample, ln_in_g, ln_in_b)
    pool_p, pool_s, ret_p, ret_s, kp, vp, ksm, vsm = [], [], [], [], [], [], [], []
    for l in range(DEPTH):
        w = (w_in[l], pool_w[l], pool_scale[l], proj_pool[l], proj_ret[l], proj_sb[l], w_out[l],
             ln_g[l], ln_b[l], sb_bias[l])
        buf0 = jnp.zeros((Bp, POOL_BUF, POOL_W), xp.dtype)
        S0 = jnp.zeros((Bp, RET_HEADS, RET_DK, RET_DV), jnp.float32)
        xp, pb, rs, k_new, v_new = mixer_layer(xp, pos_p, buf0, S0, None, None, *w)
        pool_p.append(pb); ret_p.append(rs); kp.append(k_new); vp.append(v_new)
        k_past = cache_sb_k[l][page_table].reshape(Bs, past_len, SB_HEADS, SB_DH)
        v_past = cache_sb_v[l][page_table].reshape(Bs, past_len, SB_HEADS, SB_DH)
        xs, pb, rs, k_new, v_new = mixer_layer(xs, pos_s, state_pool[l], state_ret[l], k_past, v_past, *w)
        pool_s.append(pb); ret_s.append(rs); ksm.append(k_new); vsm.append(v_new)
    return (xp, xs, jnp.stack(pool_p), jnp.stack(pool_s), jnp.stack(ret_p), jnp.stack(ret_s),
            jnp.stack(kp), jnp.stack(vp), jnp.stack(ksm), jnp.stack(vsm))
```

```python
import functools

import jax
import jax.numpy as jnp
import numpy as np
from jax import lax
from jax.experimental import pallas as pl
from jax.experimental.pallas import tpu as pltpu

F32 = jnp.float32
BF16 = jnp.bfloat16

POOL_WINDOWS = (2, 4, 8, 16)
POOL_GROUPS = len(POOL_WINDOWS)
POOL_HALO = 16
RET_HEADS = 8
RET_CHUNK = 128
RET_ROPE_BASE = 10000.0
SB_HEADS = 16
LN_EPS = 1e-5
GN_EPS = 1e-6
VMEM_LIMIT = 56 * 1024 * 1024


def _cparams(sem):
    return pltpu.CompilerParams(dimension_semantics=sem, vmem_limit_bytes=VMEM_LIMIT)


def _silu(x):
    return x * jax.nn.sigmoid(x)


def _log2(n):
    assert n & (n - 1) == 0
    return n.bit_length() - 1


def _layer_norm_rows(x, g, b):
    mu = jnp.mean(x, axis=-1, keepdims=True)
    xc = x - mu
    var = jnp.mean(xc * xc, axis=-1, keepdims=True)
    return xc * lax.rsqrt(var + LN_EPS) * g + b


def _inproj_kernel(*refs, apply_ln, k_col, v_col):
    if apply_ln:
        x_ref, g_ref, b_ref, w_ref, h_ref, k_ref, v_ref, xn_ref, xb = refs
    else:
        x_ref, w_ref, h_ref, k_ref, v_ref, xb = refs
    j = pl.program_id(1)

    @pl.when(j == 0)
    def _():
        x = x_ref[...]
        if apply_ln:
            x = _layer_norm_rows(x, g_ref[...], b_ref[...])
            xn_ref[...] = x
        xb[...] = x.astype(BF16)

    acc = jnp.dot(xb[...], w_ref[...], preferred_element_type=F32)
    h_ref[...] = acc.astype(h_ref.dtype)

    @pl.when(j == k_col)
    def _():
        k_ref[...] = acc

    @pl.when(j == v_col)
    def _():
        v_ref[...] = acc


def _inproj(x, w_bf, ln, *, k_col, v_col, tm, h_dtype):
    M, D = x.shape
    N = w_bf.shape[1]
    tn = D
    apply_ln = ln is not None
    row = pl.BlockSpec((tm, D), lambda i, j: (i, 0))
    in_specs = [row]
    args = [x]
    if apply_ln:
        vec = pl.BlockSpec((1, D), lambda i, j: (0, 0))
        in_specs += [vec, vec]
        args += [ln[0].reshape(1, D), ln[1].reshape(1, D)]
    in_specs.append(pl.BlockSpec((D, tn), lambda i, j: (0, j)))
    args.append(w_bf)
    out_shape = [jax.ShapeDtypeStruct((M, N), h_dtype), jax.ShapeDtypeStruct((M, D), F32),
                 jax.ShapeDtypeStruct((M, D), F32)]
    out_specs = [pl.BlockSpec((tm, tn), lambda i, j: (i, j)), row, row]
    if apply_ln:
        out_shape.append(jax.ShapeDtypeStruct((M, D), F32))
        out_specs.append(row)
    return pl.pallas_call(
        functools.partial(_inproj_kernel, apply_ln=apply_ln, k_col=k_col, v_col=v_col),
        name="inproj",
        grid=(M // tm, N // tn),
        in_specs=in_specs, out_specs=out_specs, out_shape=out_shape,
        scratch_shapes=[pltpu.VMEM((tm, D), BF16)],
        compiler_params=_cparams(("parallel", "arbitrary")),
    )(*args)


def _pool_kernel(u_ref, g_ref, buf_ref, pw_ref, sc_ref, a_ref, ext, *, tt, pos0, gw):
    t = pl.program_id(1)

    @pl.when(t == 0)
    def _():
        ext[0:POOL_HALO, :] = buf_ref[...]

    ext[POOL_HALO:POOL_HALO + tt, :] = u_ref[...].astype(F32)
    row = lax.broadcasted_iota(jnp.int32, (tt, gw), 0)
    pos = (pos0 + t * tt + row).astype(F32)
    for g, w in enumerate(POOL_WINDOWS):
        cols = slice(g * gw, (g + 1) * gw)
        cur = ext[POOL_HALO:POOL_HALO + tt, cols]
        acc = cur
        for i in range(1, w):
            acc = acc + ext[POOL_HALO - i:POOL_HALO - i + tt, cols]
        cnt = jnp.minimum(pos + 1.0, float(w))
        diff = (acc / cnt - cur).astype(BF16)
        mixed = jnp.dot(diff, pw_ref[g], preferred_element_type=F32)
        gate = g_ref[:, cols].astype(F32)
        a_ref[:, cols] = (mixed * sc_ref[:, cols] * _silu(gate)).astype(a_ref.dtype)
    ext[0:POOL_HALO, :] = ext[tt:tt + POOL_HALO, :]


def _pool(h, buf16, pw_bf, scale, *, B, T, tt, pos0, D):
    nt = T // tt
    gw = D // POOL_GROUPS
    return pl.pallas_call(
        functools.partial(_pool_kernel, tt=tt, pos0=pos0, gw=gw),
        name="pool",
        grid=(B, nt),
        in_specs=[pl.BlockSpec((tt, D), lambda b, t: (b * nt + t, 0)),
                  pl.BlockSpec((tt, D), lambda b, t: (b * nt + t, 1)),
                  pl.BlockSpec((None, POOL_HALO, D), lambda b, t: (b, 0, 0)),
                  pl.BlockSpec((POOL_GROUPS, gw, gw), lambda b, t: (0, 0, 0)),
                  pl.BlockSpec((1, D), lambda b, t: (0, 0))],
        out_specs=pl.BlockSpec((tt, D), lambda b, t: (b * nt + t, 0)),
        out_shape=jax.ShapeDtypeStruct((B * T, D), h.dtype),
        scratch_shapes=[pltpu.VMEM((tt + POOL_HALO, D), F32)],
        compiler_params=_cparams(("parallel", "arbitrary")),
    )(h, h, buf16, pw_bf, scale.reshape(1, D))


def _ret_kernel(*refs, has_state, dk, dv):
    if has_state:
        (gc_ref, q_ref, k_ref, v_ref, g_ref, cos_ref, sin_ref, dm_ref, qd_ref, kd_ref, s0_ref,
         b_ref, sout_ref, s_scr) = refs
    else:
        (gc_ref, q_ref, k_ref, v_ref, g_ref, cos_ref, sin_ref, dm_ref, qd_ref, kd_ref,
         b_ref, sout_ref, s_scr) = refs
    n = pl.program_id(1)

    @pl.when(n == 0)
    def _():
        if has_state:
            s_scr[...] = s0_ref[...]
        else:
            s_scr[...] = jnp.zeros_like(s_scr)

    cos_t = cos_ref[...]
    sin_t = sin_ref[...]
    nt = (((1,), (1,)), ((), ()))
    tn = (((0,), (0,)), ((), ()))
    for hd in range(RET_HEADS):
        qk = slice(hd * dk, (hd + 1) * dk)
        vv = slice(hd * dv, (hd + 1) * dv)
        qh = q_ref[:, qk].astype(F32)
        kh = k_ref[:, qk].astype(F32)
        qh = qh * cos_t + pltpu.roll(qh, dk // 2, 1) * sin_t
        kh = (kh * cos_t + pltpu.roll(kh, dk // 2, 1) * sin_t) * (dk ** -0.5)
        vb = v_ref[:, vv].astype(BF16)
        scores = lax.dot_general(qh.astype(BF16), kh.astype(BF16), nt, preferred_element_type=F32) * dm_ref[hd]
        s_h = s_scr[hd]
        o = (jnp.dot(scores.astype(BF16), vb, preferred_element_type=F32)
             + jnp.dot((qh * qd_ref[hd]).astype(BF16), s_h.astype(BF16), preferred_element_type=F32))
        s_scr[hd] = gc_ref[hd] * s_h + lax.dot_general((kh * kd_ref[hd]).astype(BF16), vb, tn,
                                                       preferred_element_type=F32)
        mu = jnp.mean(o, axis=-1, keepdims=True)
        oc = o - mu
        var = jnp.mean(oc * oc, axis=-1, keepdims=True)
        gate = g_ref[:, vv].astype(F32)
        b_ref[:, vv] = (oc * lax.rsqrt(var + GN_EPS) * _silu(gate)).astype(b_ref.dtype)

    @pl.when(n == pl.num_programs(1) - 1)
    def _():
        sout_ref[...] = s_scr[...]


def _ret_tables(pos, chunk, dk):
    half = dk // 2
    freqs = RET_ROPE_BASE ** (-jnp.arange(half, dtype=F32) / half)
    ang = pos[:, None] * freqs[None, :]
    cos, sin = jnp.cos(ang), jnp.sin(ang)
    cos_t = jnp.concatenate([cos, cos], axis=-1)
    sin_t = jnp.concatenate([-sin, sin], axis=-1)
    log_g = jnp.log1p(-(2.0 ** (-5.0 - jnp.arange(RET_HEADS, dtype=F32))))
    idx = jnp.arange(chunk, dtype=F32)
    dist = idx[:, None] - idx[None, :]
    dmask = jnp.where(dist >= 0, jnp.exp(log_g[:, None, None] * jnp.maximum(dist, 0.0)), 0.0)
    q_dec = jnp.exp(log_g[:, None] * (idx[None, :] + 1.0))
    k_dec = jnp.exp(log_g[:, None] * (chunk - 1.0 - idx[None, :]))
    q_dec = jnp.broadcast_to(q_dec[:, :, None], (RET_HEADS, chunk, dk))
    k_dec = jnp.broadcast_to(k_dec[:, :, None], (RET_HEADS, chunk, dk))
    g_chunk = jnp.exp(log_g * chunk)
    return cos_t, sin_t, dmask, q_dec, k_dec, g_chunk


def _retention(h, state, pos, *, B, T, D):
    dk = D // RET_HEADS
    dv = 2 * dk
    C = RET_CHUNK if T % RET_CHUNK == 0 else T
    nc = T // C
    cos_t, sin_t, dmask, q_dec, k_dec, g_chunk = _ret_tables(pos, C, dk)
    has_state = state is not None
    blk = lambda w, c: pl.BlockSpec((C, w), lambda b, n: (b * nc + n, c))
    full3 = lambda a: pl.BlockSpec(a.shape, lambda b, n: (0, 0, 0))
    in_specs = [pl.BlockSpec(memory_space=pltpu.SMEM),
                blk(D, 2), blk(D, 3), blk(2 * D, 2), blk(2 * D, 3),
                pl.BlockSpec((C, dk), lambda b, n: (n, 0)), pl.BlockSpec((C, dk), lambda b, n: (n, 0)),
                full3(dmask), full3(q_dec), full3(k_dec)]
    args = [g_chunk, h, h, h, h, cos_t, sin_t, dmask, q_dec, k_dec]
    st_spec = pl.BlockSpec((None, RET_HEADS, dk, dv), lambda b, n: (b, 0, 0, 0))
    if has_state:
        in_specs.append(st_spec)
        args.append(state)
    return pl.pallas_call(
        functools.partial(_ret_kernel, has_state=has_state, dk=dk, dv=dv),
        name="retention",
        grid=(B, nc),
        in_specs=in_specs,
        out_specs=[pl.BlockSpec((C, 2 * D), lambda b, n: (b * nc + n, 0)), st_spec],
        out_shape=[jax.ShapeDtypeStruct((B * T, 2 * D), h.dtype),
                   jax.ShapeDtypeStruct((B, RET_HEADS, dk, dv), F32)],
        scratch_shapes=[pltpu.VMEM((RET_HEADS, dk, dv), F32)],
        compiler_params=_cparams(("parallel", "arbitrary")),
    )(*args)


def _neg_softplus_parts(z):
    return jnp.maximum(z, 0.0) + jnp.log1p(jnp.exp(-jnp.abs(z)))


def _split_bf16(x):
    hi = x.astype(BF16)
    lo = (x - hi.astype(F32)).astype(BF16)
    return hi, lo


def _sb_prompt_kernel(bias_ref, q_ref, k_ref, v_ref, g_ref, c_ref, *, T, R, dh):
    hp = pl.program_id(1)
    scale = dh ** -0.5
    b0 = bias_ref[2 * hp]
    b1 = bias_ref[2 * hp + 1]
    lane = lax.broadcasted_iota(jnp.int32, (R, 2 * dh), 1)
    first = lane < dh
    r_i = lax.broadcasted_iota(jnp.int32, (R, R), 0)
    c_i = lax.broadcasted_iota(jnp.int32, (R, R), 1)
    later = jnp.where(r_i > c_i, 1.0, 0.0).astype(BF16)
    nt = (((1,), (1,)), ((), ()))

    def block(qs, j, carry, acc, masked):
        k0 = pl.multiple_of(j * R, R)
        kj = k_ref[pl.ds(k0, R), :]
        vj = v_ref[pl.ds(k0, R), :]
        z = lax.dot_general(qs, kj, nt, preferred_element_type=F32)
        z = jnp.concatenate([z[:R] * scale + b0, z[R:] * scale + b1], axis=0)
        sp = _neg_softplus_parts(z)
        lk = -sp
        if masked:
            causal = c_i < r_i
            causal = jnp.concatenate([causal, causal], axis=0)
            lk = jnp.where(causal, lk, 0.0)
        hi, lo = _split_bf16(lk)
        suffix = (jnp.dot(hi, later, preferred_element_type=F32)
                  + jnp.dot(lo, later, preferred_element_type=F32))
        a = jnp.exp(z - sp + suffix + carry)
        if masked:
            a = jnp.where(causal, a, 0.0)
        acc = acc + jnp.dot(a.astype(BF16), vj, preferred_element_type=F32)
        carry = carry + jnp.sum(lk, axis=1, keepdims=True)
        return carry, acc

    def q_chunk(i, _):
        r0 = pl.multiple_of(i * R, R)
        q = q_ref[pl.ds(r0, R), :].astype(F32)
        qs = jnp.concatenate([jnp.where(first, q, 0.0), jnp.where(first, 0.0, q)], axis=0).astype(BF16)
        carry = jnp.zeros((2 * R, 1), F32)
        acc = jnp.zeros((2 * R, 2 * dh), F32)
        carry, acc = block(qs, i, carry, acc, True)

        def body(jj, ca):
            return block(qs, i - jj, ca[0], ca[1], False)

        carry, acc = lax.fori_loop(1, i + 1, body, (carry, acc))
        out = jnp.where(first, acc[:R], acc[R:])
        gate = g_ref[pl.ds(r0, R), :].astype(F32)
        c_ref[pl.ds(r0, R), :] = (out * _silu(gate)).astype(BF16)
        return 0

    lax.fori_loop(0, T // R, q_chunk, 0)


def _sb_prompt(h, sb_bias, *, B, T, D, R):
    dh = D // SB_HEADS
    lanes = 2 * dh
    npair = SB_HEADS // 2
    qc, kc, vc, gc = (off // lanes for off in (8 * D, 9 * D, 10 * D, 11 * D))
    blk = lambda c0: pl.BlockSpec((T, lanes), lambda b, p: (b, c0 + p))
    return pl.pallas_call(
        functools.partial(_sb_prompt_kernel, T=T, R=R, dh=dh),
        name="sb_prompt",
        grid=(B, npair),
        in_specs=[pl.BlockSpec(memory_space=pltpu.SMEM), blk(qc), blk(kc), blk(vc), blk(gc)],
        out_specs=pl.BlockSpec((T, lanes), lambda b, p: (b, p)),
        out_shape=jax.ShapeDtypeStruct((B * T, D), BF16),
        compiler_params=_cparams(("parallel", "parallel")),
    )(sb_bias, h, h, h, h)


def _sb_decode_kernel(pt_ref, q_ref, knt_ref, vnt_ref, g_ref, bias_ref, later_ref, *rest, G, page, ts, dh):
    k_refs = rest[:G]
    v_refs = rest[G:2 * G]
    c_ref, qbd, kb, vb, acc_t, carry = rest[2 * G:]
    p = pl.program_id(1)
    nq = SB_HEADS * ts
    width = SB_HEADS * dh
    scale = dh ** -0.5
    nt = (((1,), (1,)), ((), ()))
    bias = bias_ref[...]

    @pl.when(p == 0)
    def _():
        q = q_ref[...].astype(F32)
        qt = jnp.concatenate([q] * SB_HEADS, axis=0)
        r_h = lax.broadcasted_iota(jnp.int32, (nq, width), 0) >> _log2(ts)
        c_h = lax.broadcasted_iota(jnp.int32, (nq, width), 1) >> _log2(dh)
        qbd[...] = jnp.where(r_h == c_h, qt, 0.0).astype(BF16)
        z = jnp.dot(qbd[...], knt_ref[...].astype(BF16), preferred_element_type=F32) * scale + bias
        q_i = lax.broadcasted_iota(jnp.int32, (nq, ts), 0) & (ts - 1)
        t_i = lax.broadcasted_iota(jnp.int32, (nq, ts), 1)
        vis = t_i < q_i
        sp = _neg_softplus_parts(z)
        lk = jnp.where(vis, -sp, 0.0)
        suffix = jnp.zeros((nq, ts), F32)
        for t in range(1, ts):
            suffix = suffix + jnp.where(t_i < t, lk[:, t:t + 1], 0.0)
        a = jnp.where(vis, jnp.exp(z - sp + suffix), 0.0)
        acc_t[...] = lax.dot_general(vnt_ref[...].astype(BF16), a.astype(BF16), nt, preferred_element_type=F32)
        carry[...] = jnp.sum(lk, axis=1, keepdims=True)

    for g in range(G):
        dst = slice((G - 1 - g) * page, (G - g) * page)
        kb[:, dst] = k_refs[g][...].astype(BF16)
        vb[:, dst] = v_refs[g][...].astype(BF16)
    z = jnp.dot(qbd[...], kb[...], preferred_element_type=F32) * scale + bias
    sp = _neg_softplus_parts(z)
    lk = -sp
    hi, lo = _split_bf16(lk)
    later = later_ref[...]
    suffix = (jnp.dot(hi, later, preferred_element_type=F32)
              + jnp.dot(lo, later, preferred_element_type=F32))
    a = jnp.exp(z - sp + suffix + carry[...])
    acc_t[...] += lax.dot_general(vb[...], a.astype(BF16), nt, preferred_element_type=F32)
    carry[...] += jnp.sum(lk, axis=1, keepdims=True)

    @pl.when(p == pl.num_programs(1) - 1)
    def _():
        r_h = lax.broadcasted_iota(jnp.int32, (width, nq), 0) >> _log2(dh)
        c_h = lax.broadcasted_iota(jnp.int32, (width, nq), 1) >> _log2(ts)
        own = jnp.transpose(jnp.where(r_h == c_h, acc_t[...], 0.0))
        out = own[0:ts]
        for hd in range(1, SB_HEADS):
            out = out + own[hd * ts:(hd + 1) * ts]
        gate = g_ref[...].astype(F32)
        c_ref[...] = (out * _silu(gate)).astype(c_ref.dtype)


def _sb_decode(h, cache_k, cache_v, layer, page_table, sb_bias, *, B, ts, D, G):
    dh = D // SB_HEADS
    n_pages = page_table.shape[1]
    page = cache_k.shape[3]
    nq = SB_HEADS * ts
    gp = G * page
    steps = n_pages // G
    pt_flat = page_table.reshape(-1)
    bias_col = jnp.repeat(sb_bias, ts).reshape(nq, 1)
    later = jnp.tril(jnp.ones((gp, gp), BF16), k=-1)
    h3 = h.reshape(B, ts, -1)
    knt = jnp.swapaxes(h3[:, :, 9 * D:10 * D], 1, 2)
    vnt = jnp.swapaxes(h3[:, :, 10 * D:11 * D], 1, 2)
    hblk = lambda c: pl.BlockSpec((ts, D), lambda b, p, pt: (b, c))
    newblk = pl.BlockSpec((None, D, ts), lambda b, p, pt: (b, 0, 0))

    def page_spec(g):
        return pl.BlockSpec((None, None, D, page),
                            lambda b, p, pt: (layer, pt[b * n_pages + n_pages - 1 - (p * G + g)], 0, 0))

    grid_spec = pltpu.PrefetchScalarGridSpec(
        num_scalar_prefetch=1,
        grid=(B, steps),
        in_specs=[hblk(8), newblk, newblk, hblk(11),
                  pl.BlockSpec((nq, 1), lambda b, p, pt: (0, 0)),
                  pl.BlockSpec((gp, gp), lambda b, p, pt: (0, 0))]
                 + [page_spec(g) for g in range(G)] + [page_spec(g) for g in range(G)],
        out_specs=pl.BlockSpec((ts, D), lambda b, p, pt: (b, 0)),
        scratch_shapes=[pltpu.VMEM((nq, D), BF16), pltpu.VMEM((D, gp), BF16), pltpu.VMEM((D, gp), BF16),
                        pltpu.VMEM((D, nq), F32), pltpu.VMEM((nq, 1), F32)],
    )
    return pl.pallas_call(
        functools.partial(_sb_decode_kernel, G=G, page=page, ts=ts, dh=dh),
        name="sb_decode",
        grid_spec=grid_spec,
        out_shape=jax.ShapeDtypeStruct((B * ts, D), h.dtype),
        compiler_params=_cparams(("parallel", "arbitrary")),
    )(pt_flat, h, knt, vnt, h, bias_col, later, *([cache_k] * G), *([cache_v] * G))


def _out_kernel(a_ref, b_ref, c_ref, mp_ref, mr_ref, ms_ref, x_ref, pp_ref, pr_ref, ps_ref, wo_ref,
                lg_ref, lb_ref, y_ref, *, alpha):
    def branch(act_ref, proj_ref, gate_ref):
        return jax.nn.sigmoid(gate_ref[...].astype(F32)) * jnp.dot(act_ref[...].astype(BF16), proj_ref[...],
                                                                  preferred_element_type=F32)

    merged = branch(a_ref, pp_ref, mp_ref) + branch(b_ref, pr_ref, mr_ref) + branch(c_ref, ps_ref, ms_ref)
    y = alpha * x_ref[...] + jnp.dot(merged.astype(BF16), wo_ref[...], preferred_element_type=F32)
    y_ref[...] = _layer_norm_rows(y, lg_ref[...], lb_ref[...])


def _out_stage(a, b, c, h, x, pp, pr, ps, wo, lg, lb, *, alpha, tm):
    M, D = x.shape
    row = lambda w, cb: pl.BlockSpec((tm, w), lambda i: (i, cb))
    full = lambda arr: pl.BlockSpec(arr.shape, lambda i: (0, 0))
    vec = pl.BlockSpec((1, D), lambda i: (0, 0))
    return pl.pallas_call(
        functools.partial(_out_kernel, alpha=alpha),
        name="out_stage",
        grid=(M // tm,),
        in_specs=[row(D, 0), row(2 * D, 0), row(D, 0), row(D, 12), row(D, 13), row(D, 14), row(D, 0),
                  full(pp), full(pr), full(ps), full(wo), vec, vec],
        out_specs=row(D, 0),
        out_shape=jax.ShapeDtypeStruct((M, D), F32),
        compiler_params=_cparams(("parallel",)),
    )(a, b, c, h, h, h, x, pp, pr, ps, wo, lg.reshape(1, D), lb.reshape(1, D))


def _row_tile(m, cap):
    t = min(m, cap)
    while m % t:
        t //= 2
    return t


def kernel(x_prompt, x_sample, state_pool, state_ret, cache_sb_k, cache_sb_v, page_table, ln_in_g, ln_in_b,
           w_in, pool_w, pool_scale, proj_pool, proj_ret, proj_sb, w_out, ln_g, ln_b, sb_bias):
    Bp, Tp, D = x_prompt.shape
    Bs, Ts, _ = x_sample.shape
    depth = w_in.shape[0]
    n_pages = page_table.shape[1]
    page = cache_sb_k.shape[2]
    past_len = n_pages * page
    alpha = float((2 * depth) ** 0.25)
    assert SB_HEADS * Ts == 128 and D == 1024 and Tp % 256 == 0

    pos_p = jnp.arange(Tp, dtype=F32)
    pos_s = past_len + jnp.arange(Ts, dtype=F32)
    cache_k = jnp.transpose(cache_sb_k, (0, 1, 3, 4, 2)).reshape(depth, -1, D, page)
    cache_v = jnp.transpose(cache_sb_v, (0, 1, 3, 4, 2)).reshape(depth, -1, D, page)
    G = 4 if n_pages % 4 == 0 else 1

    w_in_bf = w_in.astype(BF16)
    pool_w_bf = pool_w.astype(BF16)
    pp_bf, pr_bf, ps_bf, wo_bf = (w.astype(BF16) for w in (proj_pool, proj_ret, proj_sb, w_out))

    xp = x_prompt.reshape(Bp * Tp, D)
    xs = x_sample.reshape(Bs * Ts, D)
    tm_p = _row_tile(Bp * Tp, 1024)
    tm_s = _row_tile(Bs * Ts, 1024)
    zero_buf = jnp.zeros((Bp, POOL_HALO, D), F32)
    halo_pad = jnp.zeros((depth, Bs, POOL_HALO - state_pool.shape[2], D), F32)
    sample_buf = jnp.concatenate([halo_pad, state_pool], axis=2)
    n_keep = state_pool.shape[2]

    outs = {name: [] for name in ("pool_p", "pool_s", "ret_p", "ret_s", "kp", "vp", "ks", "vs")}
    for l in range(depth):
        ln = (ln_in_g, ln_in_b) if l == 0 else None
        res = _inproj(xp, w_in_bf[l], ln, k_col=9, v_col=10, tm=tm_p, h_dtype=BF16)
        h, k_new, v_new = res[:3]
        if l == 0:
            xp = res[3]
        a = _pool(h, zero_buf, pool_w_bf[l], pool_scale[l], B=Bp, T=Tp, tt=_row_tile(Tp, 256), pos0=0, D=D)
        b, s_new = _retention(h, None, pos_p, B=Bp, T=Tp, D=D)
        c = _sb_prompt(h, sb_bias[l], B=Bp, T=Tp, D=D, R=256)
        u_tail = h.reshape(Bp, Tp, -1)[:, Tp - n_keep:, :D].astype(F32)
        xp = _out_stage(a, b, c, h, xp, pp_bf[l], pr_bf[l], ps_bf[l], wo_bf[l], ln_g[l], ln_b[l],
                        alpha=alpha, tm=_row_tile(Bp * Tp, 512))
        outs["pool_p"].append(u_tail); outs["ret_p"].append(s_new)
        outs["kp"].append(k_new.reshape(Bp, Tp, SB_HEADS, -1)); outs["vp"].append(v_new.reshape(Bp, Tp, SB_HEADS, -1))
        res = _inproj(xs, w_in_bf[l], ln, k_col=9, v_col=10, tm=tm_s, h_dtype=F32)
        h, k_new, v_new = res[:3]
        if l == 0:
            xs = res[3]
        a = _pool(h, sample_buf[l], pool_w_bf[l], pool_scale[l], B=Bs, T=Ts, tt=Ts, pos0=past_len, D=D)
        b, s_new = _retention(h, state_ret[l], pos_s, B=Bs, T=Ts, D=D)
        c = _sb_decode(h, cache_k, cache_v, l, page_table, sb_bias[l], B=Bs, ts=Ts, D=D, G=G)
        u_new = h.reshape(Bs, Ts, -1)[:, :, :D].astype(F32)
        u_tail = jnp.concatenate([state_pool[l], u_new], axis=1)[:, -n_keep:]
        xs = _out_stage(a, b, c, h, xs, pp_bf[l], pr_bf[l], ps_bf[l], wo_bf[l], ln_g[l], ln_b[l],
                        alpha=alpha, tm=tm_s)
        outs["pool_s"].append(u_tail); outs["ret_s"].append(s_new)
        outs["ks"].append(k_new.reshape(Bs, Ts, SB_HEADS, -1)); outs["vs"].append(v_new.reshape(Bs, Ts, SB_HEADS, -1))

    st = lambda name: jnp.stack(outs[name])
    return (xp.reshape(Bp, Tp, D), xs.reshape(Bs, Ts, D), st("pool_p"), st("pool_s"), st("ret_p"), st("ret_s"),
            st("kp"), st("vp"), st("ks"), st("vs"))
```

```python
import functools

import jax
import jax.numpy as jnp
import numpy as np
from jax import lax
from jax.experimental import pallas as pl
from jax.experimental.pallas import tpu as pltpu

F32 = jnp.float32
BF16 = jnp.bfloat16

POOL_WINDOWS = (2, 4, 8, 16)
POOL_GROUPS = len(POOL_WINDOWS)
POOL_HALO = 16
RET_HEADS = 8
RET_CHUNK = 128
RET_ROPE_BASE = 10000.0
SB_HEADS = 16
LN_EPS = 1e-5
GN_EPS = 1e-6
VMEM_LIMIT = 56 * 1024 * 1024


def _cparams(sem):
    return pltpu.CompilerParams(dimension_semantics=sem, vmem_limit_bytes=VMEM_LIMIT)


def _silu(x):
    return x * jax.nn.sigmoid(x)


def _log2(n):
    assert n & (n - 1) == 0
    return n.bit_length() - 1


def _layer_norm_rows(x, g, b):
    mu = jnp.mean(x, axis=-1, keepdims=True)
    xc = x - mu
    var = jnp.mean(xc * xc, axis=-1, keepdims=True)
    return xc * lax.rsqrt(var + LN_EPS) * g + b


def _inproj_kernel(*refs, apply_ln, k_col, v_col):
    if apply_ln:
        x_ref, g_ref, b_ref, w_ref, _k_all_in, _v_all_in, h_ref, k_ref, v_ref, xn_ref, xb = refs
    else:
        x_ref, w_ref, _k_all_in, _v_all_in, h_ref, k_ref, v_ref, xb = refs
    j = pl.program_id(1)

    @pl.when(j == 0)
    def _():
        x = x_ref[...]
        if apply_ln:
            x = _layer_norm_rows(x, g_ref[...], b_ref[...])
            xn_ref[...] = x
        xb[...] = x.astype(BF16)

    acc = jnp.dot(xb[...], w_ref[...], preferred_element_type=F32)
    h_ref[...] = acc.astype(h_ref.dtype)

    @pl.when(j == k_col)
    def _():
        k_ref[...] = acc

    @pl.when(j == v_col)
    def _():
        v_ref[...] = acc


def _inproj(x, w_bf, ln, k_all, v_all, layer, *, k_col, v_col, tm, h_dtype):
    M, D = x.shape
    N = w_bf.shape[1]
    tn = D
    apply_ln = ln is not None
    row = pl.BlockSpec((tm, D), lambda i, j: (i, 0))
    in_specs = [row]
    args = [x]
    if apply_ln:
        vec = pl.BlockSpec((1, D), lambda i, j: (0, 0))
        in_specs += [vec, vec]
        args += [ln[0].reshape(1, D), ln[1].reshape(1, D)]
    in_specs.append(pl.BlockSpec((D, tn), lambda i, j: (0, j)))
    args.append(w_bf)
    n_in = len(args)
    in_specs += [pl.BlockSpec(memory_space=pl.ANY)] * 2
    args += [k_all, v_all]
    layer_row = pl.BlockSpec((None, tm, D), lambda i, j: (layer, i, 0))
    out_shape = [jax.ShapeDtypeStruct((M, N), h_dtype), jax.ShapeDtypeStruct(k_all.shape, F32),
                 jax.ShapeDtypeStruct(v_all.shape, F32)]
    out_specs = [pl.BlockSpec((tm, tn), lambda i, j: (i, j)), layer_row, layer_row]
    if apply_ln:
        out_shape.append(jax.ShapeDtypeStruct((M, D), F32))
        out_specs.append(row)
    return pl.pallas_call(
        functools.partial(_inproj_kernel, apply_ln=apply_ln, k_col=k_col, v_col=v_col),
        name="inproj",
        grid=(M // tm, N // tn),
        in_specs=in_specs, out_specs=out_specs, out_shape=out_shape,
        input_output_aliases={n_in: 1, n_in + 1: 2},
        scratch_shapes=[pltpu.VMEM((tm, D), BF16)],
        compiler_params=_cparams(("parallel", "arbitrary")),
    )(*args)


def _pool_kernel(u_ref, g_ref, buf_ref, pw_ref, sc_ref, a_ref, ext, *, tt, pos0, gw):
    t = pl.program_id(1)

    @pl.when(t == 0)
    def _():
        ext[0:POOL_HALO, :] = buf_ref[...]

    ext[POOL_HALO:POOL_HALO + tt, :] = u_ref[...].astype(F32)
    row = lax.broadcasted_iota(jnp.int32, (tt, gw), 0)
    pos = (pos0 + t * tt + row).astype(F32)
    for g, w in enumerate(POOL_WINDOWS):
        cols = slice(g * gw, (g + 1) * gw)
        cur = ext[POOL_HALO:POOL_HALO + tt, cols]
        acc = cur
        for i in range(1, w):
            acc = acc + ext[POOL_HALO - i:POOL_HALO - i + tt, cols]
        cnt = jnp.minimum(pos + 1.0, float(w))
        diff = (acc / cnt - cur).astype(BF16)
        mixed = jnp.dot(diff, pw_ref[g], preferred_element_type=F32)
        gate = g_ref[:, cols].astype(F32)
        a_ref[:, cols] = (mixed * sc_ref[:, cols] * _silu(gate)).astype(a_ref.dtype)
    ext[0:POOL_HALO, :] = ext[tt:tt + POOL_HALO, :]


def _pool(h, buf16, pw_bf, scale, *, B, T, tt, pos0, D):
    nt = T // tt
    gw = D // POOL_GROUPS
    return pl.pallas_call(
        functools.partial(_pool_kernel, tt=tt, pos0=pos0, gw=gw),
        name="pool",
        grid=(B, nt),
        in_specs=[pl.BlockSpec((tt, D), lambda b, t: (b * nt + t, 0)),
                  pl.BlockSpec((tt, D), lambda b, t: (b * nt + t, 1)),
                  pl.BlockSpec((None, POOL_HALO, D), lambda b, t: (b, 0, 0)),
                  pl.BlockSpec((POOL_GROUPS, gw, gw), lambda b, t: (0, 0, 0)),
                  pl.BlockSpec((1, D), lambda b, t: (0, 0))],
        out_specs=pl.BlockSpec((tt, D), lambda b, t: (b * nt + t, 0)),
        out_shape=jax.ShapeDtypeStruct((B * T, D), h.dtype),
        scratch_shapes=[pltpu.VMEM((tt + POOL_HALO, D), F32)],
        compiler_params=_cparams(("parallel", "arbitrary")),
    )(h, h, buf16, pw_bf, scale.reshape(1, D))


def _ret_kernel(*refs, has_state, dk, dv):
    if has_state:
        (gc_ref, q_ref, k_ref, v_ref, g_ref, cos_ref, sin_ref, dm_ref, qd_ref, kd_ref, s0_ref,
         b_ref, sout_ref, s_scr) = refs
    else:
        (gc_ref, q_ref, k_ref, v_ref, g_ref, cos_ref, sin_ref, dm_ref, qd_ref, kd_ref,
         b_ref, sout_ref, s_scr) = refs
    n = pl.program_id(1)

    @pl.when(n == 0)
    def _():
        if has_state:
            s_scr[...] = s0_ref[...]
        else:
            s_scr[...] = jnp.zeros_like(s_scr)

    cos_t = cos_ref[...]
    sin_t = sin_ref[...]
    nt = (((1,), (1,)), ((), ()))
    tn = (((0,), (0,)), ((), ()))
    for hd in range(RET_HEADS):
        qk = slice(hd * dk, (hd + 1) * dk)
        vv = slice(hd * dv, (hd + 1) * dv)
        qh = q_ref[:, qk].astype(F32)
        kh = k_ref[:, qk].astype(F32)
        qh = qh * cos_t + pltpu.roll(qh, dk // 2, 1) * sin_t
        kh = (kh * cos_t + pltpu.roll(kh, dk // 2, 1) * sin_t) * (dk ** -0.5)
        vb = v_ref[:, vv].astype(BF16)
        scores = lax.dot_general(qh.astype(BF16), kh.astype(BF16), nt, preferred_element_type=F32) * dm_ref[hd]
        s_h = s_scr[hd]
        o = (jnp.dot(scores.astype(BF16), vb, preferred_element_type=F32)
             + jnp.dot((qh * qd_ref[hd]).astype(BF16), s_h.astype(BF16), preferred_element_type=F32))
        s_scr[hd] = gc_ref[hd] * s_h + lax.dot_general((kh * kd_ref[hd]).astype(BF16), vb, tn,
                                                       preferred_element_type=F32)
        mu = jnp.mean(o, axis=-1, keepdims=True)
        oc = o - mu
        var = jnp.mean(oc * oc, axis=-1, keepdims=True)
        gate = g_ref[:, vv].astype(F32)
        b_ref[:, vv] = (oc * lax.rsqrt(var + GN_EPS) * _silu(gate)).astype(b_ref.dtype)

    @pl.when(n == pl.num_programs(1) - 1)
    def _():
        sout_ref[...] = s_scr[...]


def _ret_tables(pos, chunk, dk):
    half = dk // 2
    freqs = RET_ROPE_BASE ** (-jnp.arange(half, dtype=F32) / half)
    ang = pos[:, None] * freqs[None, :]
    cos, sin = jnp.cos(ang), jnp.sin(ang)
    cos_t = jnp.concatenate([cos, cos], axis=-1)
    sin_t = jnp.concatenate([-sin, sin], axis=-1)
    log_g = jnp.log1p(-(2.0 ** (-5.0 - jnp.arange(RET_HEADS, dtype=F32))))
    idx = jnp.arange(chunk, dtype=F32)
    dist = idx[:, None] - idx[None, :]
    dmask = jnp.where(dist >= 0, jnp.exp(log_g[:, None, None] * jnp.maximum(dist, 0.0)), 0.0)
    q_dec = jnp.exp(log_g[:, None] * (idx[None, :] + 1.0))
    k_dec = jnp.exp(log_g[:, None] * (chunk - 1.0 - idx[None, :]))
    q_dec = jnp.broadcast_to(q_dec[:, :, None], (RET_HEADS, chunk, dk))
    k_dec = jnp.broadcast_to(k_dec[:, :, None], (RET_HEADS, chunk, dk))
    g_chunk = jnp.exp(log_g * chunk)
    return cos_t, sin_t, dmask, q_dec, k_dec, g_chunk


def _retention(h, state, pos, *, B, T, D):
    dk = D // RET_HEADS
    dv = 2 * dk
    C = RET_CHUNK if T % RET_CHUNK == 0 else T
    nc = T // C
    cos_t, sin_t, dmask, q_dec, k_dec, g_chunk = _ret_tables(pos, C, dk)
    has_state = state is not None
    blk = lambda w, c: pl.BlockSpec((C, w), lambda b, n: (b * nc + n, c))
    full3 = lambda a: pl.BlockSpec(a.shape, lambda b, n: (0, 0, 0))
    in_specs = [pl.BlockSpec(memory_space=pltpu.SMEM),
                blk(D, 2), blk(D, 3), blk(2 * D, 2), blk(2 * D, 3),
                pl.BlockSpec((C, dk), lambda b, n: (n, 0)), pl.BlockSpec((C, dk), lambda b, n: (n, 0)),
                full3(dmask), full3(q_dec), full3(k_dec)]
    args = [g_chunk, h, h, h, h, cos_t, sin_t, dmask, q_dec, k_dec]
    st_spec = pl.BlockSpec((None, RET_HEADS, dk, dv), lambda b, n: (b, 0, 0, 0))
    if has_state:
        in_specs.append(st_spec)
        args.append(state)
    return pl.pallas_call(
        functools.partial(_ret_kernel, has_state=has_state, dk=dk, dv=dv),
        name="retention",
        grid=(B, nc),
        in_specs=in_specs,
        out_specs=[pl.BlockSpec((C, 2 * D), lambda b, n: (b * nc + n, 0)), st_spec],
        out_shape=[jax.ShapeDtypeStruct((B * T, 2 * D), h.dtype),
                   jax.ShapeDtypeStruct((B, RET_HEADS, dk, dv), F32)],
        scratch_shapes=[pltpu.VMEM((RET_HEADS, dk, dv), F32)],
        compiler_params=_cparams(("parallel", "arbitrary")),
    )(*args)


MASKED = -1e30


def _softplus(z):
    return jnp.maximum(z, 0.0) + jnp.log(1.0 + jnp.exp(-jnp.abs(z)))


def _sb_prompt_kernel(tab_ref, bias_ref, q_ref, k_ref, v_ref, g_ref, c_ref,
                      z_scr, e0_scr, spb_scr, suf_scr, a_scr, acc_scr, carry_scr, *, n_blocks, n_chunks, R, dh):
    hp = pl.program_id(1)
    scale = dh ** -0.5
    b0 = bias_ref[2 * hp]
    b1 = bias_ref[2 * hp + 1]
    first = lax.broadcasted_iota(jnp.int32, (R, 2 * dh), 1) < dh
    r_i = lax.broadcasted_iota(jnp.int32, (R, R), 0)
    c_i = lax.broadcasted_iota(jnp.int32, (R, R), 1)
    later = jnp.where(r_i > c_i, 1.0, 0.0).astype(BF16)
    nt = (((1,), (1,)), ((), ()))

    def rows(b):
        return pl.multiple_of(tab_ref[0, b] * R, R)

    def keys(b):
        return pl.multiple_of(tab_ref[1, b] * R, R)

    def scores_stage(b, s):
        q = q_ref[pl.ds(rows(b), R), :].astype(F32) * scale
        qs = jnp.concatenate([jnp.where(first, q, 0.0), jnp.where(first, 0.0, q)], axis=0).astype(BF16)
        z_scr[s] = lax.dot_general(qs, k_ref[pl.ds(keys(b), R), :], nt, preferred_element_type=F32)

    def log_stage(b, s):
        i, j = tab_ref[0, b], tab_ref[1, b]
        z = z_scr[s]
        visible = (c_i - r_i) < (i - j) * R
        z = jnp.concatenate([jnp.where(visible, z[:R] + b0, MASKED), jnp.where(visible, z[R:] + b1, MASKED)],
                            axis=0)
        sp = _softplus(z)
        before = jnp.where(j == i, 0.0, carry_scr[...])
        e0_scr[s] = z - sp - before
        spb_scr[s] = sp.astype(BF16)
        carry_scr[...] = before + jnp.sum(sp, axis=1, keepdims=True)

    def suffix_stage(s):
        suf_scr[s] = jnp.dot(spb_scr[s], later, preferred_element_type=F32)

    def weight_stage(s):
        a_scr[s] = jnp.exp(e0_scr[s] - suf_scr[s]).astype(BF16)

    def value_stage(b, s):
        acc_scr[tab_ref[0, b]] += jnp.dot(a_scr[s], v_ref[pl.ds(keys(b), R), :], preferred_element_type=F32)

    def step(h, p, valid):
        if valid(h - 2):
            suffix_stage(p)
        if valid(h - 4):
            value_stage(h - 4, p)
        if valid(h - 3):
            weight_stage(1 - p)
        if valid(h - 1):
            log_stage(h - 1, 1 - p)
        if valid(h):
            scores_stage(h, p)

    depth = 4
    in_range = lambda b: 0 <= b < n_blocks
    always = lambda b: True
    acc_scr[...] = jnp.zeros_like(acc_scr)
    carry_scr[...] = jnp.zeros_like(carry_scr)
    for h in range(min(depth, n_blocks + depth)):
        step(h, h % 2, in_range)
    unroll = 2
    n_iters = max(n_blocks - depth, 0) // unroll

    def steady_steps(t, carry):
        h = depth + unroll * t
        for k in range(unroll):
            step(h + k, (depth + k) % 2, always)
        return carry

    lax.fori_loop(0, n_iters, steady_steps, 0)
    for h in range(depth + unroll * n_iters, n_blocks + depth):
        step(h, h % 2, in_range)

    def write_chunk(i, carry):
        r0 = pl.multiple_of(i * R, R)
        acc = acc_scr[i]
        out = jnp.where(first, acc[:R], acc[R:])
        gate = g_ref[pl.ds(r0, R), :].astype(F32)
        c_ref[pl.ds(r0, R), :] = (out * _silu(gate)).astype(c_ref.dtype)
        return carry

    lax.fori_loop(0, n_chunks, write_chunk, 0)


def _sb_prompt(h, sb_bias, *, B, T, D, R):
    dh = D // SB_HEADS
    lanes = 2 * dh
    npair = SB_HEADS // 2
    n_chunks = T // R
    order = [(i, j) for i in range(n_chunks) for j in range(i, -1, -1)]
    table = jnp.asarray(np.array(order, np.int32).T)
    qc, kc, vc, gc = (off // lanes for off in (8 * D, 9 * D, 10 * D, 11 * D))
    blk = lambda c0: pl.BlockSpec((T, lanes), lambda b, p: (b, c0 + p))
    smem = pl.BlockSpec(memory_space=pltpu.SMEM)
    return pl.pallas_call(
        functools.partial(_sb_prompt_kernel, n_blocks=len(order), n_chunks=n_chunks, R=R, dh=dh),
        name="sb_prompt",
        grid=(B, npair),
        in_specs=[smem, smem, blk(qc), blk(kc), blk(vc), blk(gc)],
        out_specs=pl.BlockSpec((T, lanes), lambda b, p: (b, p)),
        out_shape=jax.ShapeDtypeStruct((B * T, D), BF16),
        scratch_shapes=[pltpu.VMEM((2, 2 * R, R), F32), pltpu.VMEM((2, 2 * R, R), F32),
                        pltpu.VMEM((2, 2 * R, R), BF16), pltpu.VMEM((2, 2 * R, R), F32),
                        pltpu.VMEM((2, 2 * R, R), BF16), pltpu.VMEM((n_chunks, 2 * R, lanes), F32),
                        pltpu.VMEM((2 * R, 1), F32)],
        compiler_params=_cparams(("parallel", "parallel")),
    )(table, sb_bias, h, h, h, h)


def _sb_decode_kernel(pt_ref, q_ref, knt_ref, vnt_ref, g_ref, bias_ref, later_ref, *rest, G, sub, ts, dh):
    k_refs = rest[:G]
    v_refs = rest[G:2 * G]
    c_ref, qbd, acc_t, carry = rest[2 * G:]
    p = pl.program_id(1)
    nq = SB_HEADS * ts
    width = SB_HEADS * dh
    scale = dh ** -0.5
    nt = (((1,), (1,)), ((), ()))
    bias = bias_ref[...]

    @pl.when(p == 0)
    def _():
        q = q_ref[...].astype(F32) * scale
        qt = jnp.concatenate([q] * SB_HEADS, axis=0)
        r_h = lax.broadcasted_iota(jnp.int32, (nq, width), 0) >> _log2(ts)
        c_h = lax.broadcasted_iota(jnp.int32, (nq, width), 1) >> _log2(dh)
        qbd[...] = jnp.where(r_h == c_h, qt, 0.0)
        z = jnp.dot(qbd[...], knt_ref[...].astype(F32), preferred_element_type=F32) + bias
        q_i = lax.broadcasted_iota(jnp.int32, (nq, ts), 0) & (ts - 1)
        t_i = lax.broadcasted_iota(jnp.int32, (nq, ts), 1)
        z = jnp.where(t_i < q_i, z, MASKED)
        sp = _softplus(z)
        suffix = jnp.zeros((nq, ts), F32)
        for t in range(1, ts):
            suffix = suffix + jnp.where(t_i < t, sp[:, t:t + 1], 0.0)
        a = jnp.exp(z - sp - suffix)
        acc_t[...] = lax.dot_general(vnt_ref[...].astype(F32), a, nt, preferred_element_type=F32)
        carry[...] = jnp.sum(sp, axis=1, keepdims=True)

    later = later_ref[...]
    for s in range(G // sub):
        pages = [s * sub + g for g in reversed(range(sub))]
        z = jnp.dot(qbd[...], jnp.concatenate([k_refs[g][...] for g in pages], axis=1),
                    preferred_element_type=F32) + bias
        sp = _softplus(z)
        suffix = jnp.dot(sp.astype(BF16), later, preferred_element_type=F32)
        a = jnp.exp(z - sp - suffix - carry[...])
        acc_t[...] += lax.dot_general(jnp.concatenate([v_refs[g][...] for g in pages], axis=1), a, nt,
                                      preferred_element_type=F32)
        carry[...] += jnp.sum(sp, axis=1, keepdims=True)

    @pl.when(p == pl.num_programs(1) - 1)
    def _():
        r_h = lax.broadcasted_iota(jnp.int32, (width, nq), 0) >> _log2(dh)
        c_h = lax.broadcasted_iota(jnp.int32, (width, nq), 1) >> _log2(ts)
        own = jnp.transpose(jnp.where(r_h == c_h, acc_t[...], 0.0))
        out = own[0:ts]
        for hd in range(1, SB_HEADS):
            out = out + own[hd * ts:(hd + 1) * ts]
        gate = g_ref[...].astype(F32)
        c_ref[...] = (out * _silu(gate)).astype(c_ref.dtype)


def _sb_decode(h, cache_k, cache_v, layer, page_table, sb_bias, *, B, ts, D, G, sub):
    dh = D // SB_HEADS
    n_pages = page_table.shape[1]
    page = cache_k.shape[3]
    nq = SB_HEADS * ts
    gp = sub * page
    steps = n_pages // G
    pt_flat = page_table.reshape(-1)
    bias_col = jnp.repeat(sb_bias, ts).reshape(nq, 1)
    later = jnp.tril(jnp.ones((gp, gp), BF16), k=-1)
    h3 = h.reshape(B, ts, -1)
    knt = jnp.swapaxes(h3[:, :, 9 * D:10 * D], 1, 2)
    vnt = jnp.swapaxes(h3[:, :, 10 * D:11 * D], 1, 2)
    hblk = lambda c: pl.BlockSpec((ts, D), lambda b, p, pt: (b, c))
    newblk = pl.BlockSpec((None, D, ts), lambda b, p, pt: (b, 0, 0))

    def page_spec(g):
        return pl.BlockSpec((None, None, D, page),
                            lambda b, p, pt: (layer, pt[b * n_pages + n_pages - 1 - (p * G + g)], 0, 0))

    grid_spec = pltpu.PrefetchScalarGridSpec(
        num_scalar_prefetch=1,
        grid=(B, steps),
        in_specs=[hblk(8), newblk, newblk, hblk(11),
                  pl.BlockSpec((nq, 1), lambda b, p, pt: (0, 0)),
                  pl.BlockSpec((gp, gp), lambda b, p, pt: (0, 0))]
                 + [page_spec(g) for g in range(G)] + [page_spec(g) for g in range(G)],
        out_specs=pl.BlockSpec((ts, D), lambda b, p, pt: (b, 0)),
        scratch_shapes=[pltpu.VMEM((nq, D), F32), pltpu.VMEM((D, nq), F32), pltpu.VMEM((nq, 1), F32)],
    )
    return pl.pallas_call(
        functools.partial(_sb_decode_kernel, G=G, sub=sub, ts=ts, dh=dh),
        name="sb_decode",
        grid_spec=grid_spec,
        out_shape=jax.ShapeDtypeStruct((B * ts, D), h.dtype),
        compiler_params=_cparams(("parallel", "arbitrary")),
    )(pt_flat, h, knt, vnt, h, bias_col, later, *([cache_k] * G), *([cache_v] * G))


def _out_kernel(a_ref, b_ref, c_ref, mp_ref, mr_ref, ms_ref, x_ref, pp_ref, pr_ref, ps_ref, wo_ref,
                lg_ref, lb_ref, y_ref, *, alpha):
    def branch(act_ref, proj_ref, gate_ref):
        return jax.nn.sigmoid(gate_ref[...].astype(F32)) * jnp.dot(act_ref[...].astype(BF16), proj_ref[...],
                                                                  preferred_element_type=F32)

    merged = branch(a_ref, pp_ref, mp_ref) + branch(b_ref, pr_ref, mr_ref) + branch(c_ref, ps_ref, ms_ref)
    y = alpha * x_ref[...] + jnp.dot(merged.astype(BF16), wo_ref[...], preferred_element_type=F32)
    y_ref[...] = _layer_norm_rows(y, lg_ref[...], lb_ref[...])


def _out_stage(a, b, c, h, x, pp, pr, ps, wo, lg, lb, *, alpha, tm):
    M, D = x.shape
    row = lambda w, cb: pl.BlockSpec((tm, w), lambda i: (i, cb))
    full = lambda arr: pl.BlockSpec(arr.shape, lambda i: (0, 0))
    vec = pl.BlockSpec((1, D), lambda i: (0, 0))
    return pl.pallas_call(
        functools.partial(_out_kernel, alpha=alpha),
        name="out_stage",
        grid=(M // tm,),
        in_specs=[row(D, 0), row(2 * D, 0), row(D, 0), row(D, 12), row(D, 13), row(D, 14), row(D, 0),
                  full(pp), full(pr), full(ps), full(wo), vec, vec],
        out_specs=row(D, 0),
        out_shape=jax.ShapeDtypeStruct((M, D), F32),
        compiler_params=_cparams(("parallel",)),
    )(a, b, c, h, h, h, x, pp, pr, ps, wo, lg.reshape(1, D), lb.reshape(1, D))


def _row_tile(m, cap):
    t = min(m, cap)
    while m % t:
        t //= 2
    return t


def kernel(x_prompt, x_sample, state_pool, state_ret, cache_sb_k, cache_sb_v, page_table, ln_in_g, ln_in_b,
           w_in, pool_w, pool_scale, proj_pool, proj_ret, proj_sb, w_out, ln_g, ln_b, sb_bias):
    Bp, Tp, D = x_prompt.shape
    Bs, Ts, _ = x_sample.shape
    depth = w_in.shape[0]
    n_pages = page_table.shape[1]
    page = cache_sb_k.shape[2]
    past_len = n_pages * page
    alpha = float((2 * depth) ** 0.25)
    assert SB_HEADS * Ts == 128 and D == 1024 and Tp % 256 == 0

    pos_p = jnp.arange(Tp, dtype=F32)
    pos_s = past_len + jnp.arange(Ts, dtype=F32)
    cache_k = jnp.transpose(cache_sb_k, (0, 1, 3, 4, 2)).reshape(depth, -1, D, page)
    cache_v = jnp.transpose(cache_sb_v, (0, 1, 3, 4, 2)).reshape(depth, -1, D, page)
    sub = 4 if n_pages % 4 == 0 else 1
    G = 2 * sub if n_pages % (2 * sub) == 0 else sub

    w_in_bf = w_in.astype(BF16)
    pool_w_bf = pool_w.astype(BF16)
    pp_bf, pr_bf, ps_bf, wo_bf = (w.astype(BF16) for w in (proj_pool, proj_ret, proj_sb, w_out))

    xp = x_prompt.reshape(Bp * Tp, D)
    xs = x_sample.reshape(Bs * Ts, D)
    tm_p = _row_tile(Bp * Tp, 1024)
    tm_s = _row_tile(Bs * Ts, 1024)
    zero_buf = jnp.zeros((Bp, POOL_HALO, D), F32)
    halo_pad = jnp.zeros((depth, Bs, POOL_HALO - state_pool.shape[2], D), F32)
    sample_buf = jnp.concatenate([halo_pad, state_pool], axis=2)
    n_keep = state_pool.shape[2]

    outs = {name: [] for name in ("pool_p", "pool_s", "ret_p", "ret_s")}
    kp_all, vp_all = (jnp.zeros((depth, Bp * Tp, D), F32) for _ in range(2))
    ks_all, vs_all = (jnp.zeros((depth, Bs * Ts, D), F32) for _ in range(2))
    for l in range(depth):
        ln = (ln_in_g, ln_in_b) if l == 0 else None
        res = _inproj(xp, w_in_bf[l], ln, kp_all, vp_all, l, k_col=9, v_col=10, tm=tm_p, h_dtype=BF16)
        h, kp_all, vp_all = res[:3]
        if l == 0:
            xp = res[3]
        a = _pool(h, zero_buf, pool_w_bf[l], pool_scale[l], B=Bp, T=Tp, tt=_row_tile(Tp, 256), pos0=0, D=D)
        b, s_new = _retention(h, None, pos_p, B=Bp, T=Tp, D=D)
        c = _sb_prompt(h, sb_bias[l], B=Bp, T=Tp, D=D, R=256)
        u_tail = h.reshape(Bp, Tp, -1)[:, Tp - n_keep:, :D].astype(F32)
        xp = _out_stage(a, b, c, h, xp, pp_bf[l], pr_bf[l], ps_bf[l], wo_bf[l], ln_g[l], ln_b[l],
                        alpha=alpha, tm=_row_tile(Bp * Tp, 512))
        outs["pool_p"].append(u_tail); outs["ret_p"].append(s_new)
        res = _inproj(xs, w_in_bf[l], ln, ks_all, vs_all, l, k_col=9, v_col=10, tm=tm_s, h_dtype=F32)
        h, ks_all, vs_all = res[:3]
        if l == 0:
            xs = res[3]
        a = _pool(h, sample_buf[l], pool_w_bf[l], pool_scale[l], B=Bs, T=Ts, tt=Ts, pos0=past_len, D=D)
        b, s_new = _retention(h, state_ret[l], pos_s, B=Bs, T=Ts, D=D)
        c = _sb_decode(h, cache_k, cache_v, l, page_table, sb_bias[l], B=Bs, ts=Ts, D=D, G=G, sub=sub)
        u_new = h.reshape(Bs, Ts, -1)[:, :, :D].astype(F32)
        u_tail = jnp.concatenate([state_pool[l], u_new], axis=1)[:, -n_keep:]
        xs = _out_stage(a, b, c, h, xs, pp_bf[l], pr_bf[l], ps_bf[l], wo_bf[l], ln_g[l], ln_b[l],
                        alpha=alpha, tm=tm_s)
        outs["pool_s"].append(u_tail); outs["ret_s"].append(s_new)

    st = lambda name: jnp.stack(outs[name])
    heads = lambda kv, B, T: kv.reshape(depth, B, T, SB_HEADS, D // SB_HEADS)
    return (xp.reshape(Bp, Tp, D), xs.reshape(Bs, Ts, D), st("pool_p"), st("pool_s"), st("ret_p"), st("ret_s"),
            heads(kp_all, Bp, Tp), heads(vp_all, Bp, Tp), heads(ks_all, Bs, Ts), heads(vs_all, Bs, Ts))
```

```python
import functools

import jax
import jax.numpy as jnp
import numpy as np
from jax import lax
from jax.experimental import pallas as pl
from jax.experimental.pallas import tpu as pltpu

F32 = jnp.float32
BF16 = jnp.bfloat16

POOL_WINDOWS = (2, 4, 8, 16)
POOL_GROUPS = len(POOL_WINDOWS)
POOL_HALO = 16
RET_HEADS = 8
RET_CHUNK = 128
RET_ROPE_BASE = 10000.0
SB_HEADS = 16
LN_EPS = 1e-5
GN_EPS = 1e-6
VMEM_LIMIT = 56 * 1024 * 1024


def _cparams(sem):
    return pltpu.CompilerParams(dimension_semantics=sem, vmem_limit_bytes=VMEM_LIMIT)


def _silu(x):
    return x * jax.nn.sigmoid(x)


def _log2(n):
    assert n & (n - 1) == 0
    return n.bit_length() - 1


def _layer_norm_rows(x, g, b):
    mu = jnp.mean(x, axis=-1, keepdims=True)
    xc = x - mu
    var = jnp.mean(xc * xc, axis=-1, keepdims=True)
    return xc * lax.rsqrt(var + LN_EPS) * g + b


def _inproj_kernel(*refs, apply_ln, k_col, v_col):
    if apply_ln:
        x_ref, g_ref, b_ref, w_ref, _k_all_in, _v_all_in, h_ref, k_ref, v_ref, xn_ref, xb = refs
    else:
        x_ref, w_ref, _k_all_in, _v_all_in, h_ref, k_ref, v_ref, xb = refs
    j = pl.program_id(1)

    @pl.when(j == 0)
    def _():
        x = x_ref[...]
        if apply_ln:
            x = _layer_norm_rows(x, g_ref[...], b_ref[...])
            xn_ref[...] = x
        xb[...] = x.astype(BF16)

    acc = jnp.dot(xb[...], w_ref[...], preferred_element_type=F32)
    h_ref[...] = acc.astype(h_ref.dtype)

    def emit(ref):
        ref[...] = acc if ref.ndim == 2 else acc.T.reshape(ref.shape)

    @pl.when(j == k_col)
    def _():
        emit(k_ref)

    @pl.when(j == v_col)
    def _():
        emit(v_ref)


def _inproj(x, w_bf, ln, k_all, v_all, layer, *, k_col, v_col, tm, h_dtype):
    M, D = x.shape
    N = w_bf.shape[1]
    tn = D
    apply_ln = ln is not None
    row = pl.BlockSpec((tm, D), lambda i, j: (i, 0))
    in_specs = [row]
    args = [x]
    if apply_ln:
        vec = pl.BlockSpec((1, D), lambda i, j: (0, 0))
        in_specs += [vec, vec]
        args += [ln[0].reshape(1, D), ln[1].reshape(1, D)]
    in_specs.append(pl.BlockSpec((D, tn), lambda i, j: (0, j)))
    args.append(w_bf)
    n_in = len(args)
    in_specs += [pl.BlockSpec(memory_space=pl.ANY)] * 2
    args += [k_all, v_all]
    if k_all.ndim == 3:
        layer_row = pl.BlockSpec((None, tm, D), lambda i, j: (layer, i, 0))
    else:
        _, _, heads, dh, T = k_all.shape
        per_seq = T // tm
        layer_row = pl.BlockSpec((None, None, heads, dh, tm),
                                 lambda i, j: (layer, i // per_seq, 0, 0, i % per_seq))
    out_shape = [jax.ShapeDtypeStruct((M, N), h_dtype), jax.ShapeDtypeStruct(k_all.shape, F32),
                 jax.ShapeDtypeStruct(v_all.shape, F32)]
    out_specs = [pl.BlockSpec((tm, tn), lambda i, j: (i, j)), layer_row, layer_row]
    if apply_ln:
        out_shape.append(jax.ShapeDtypeStruct((M, D), F32))
        out_specs.append(row)
    return pl.pallas_call(
        functools.partial(_inproj_kernel, apply_ln=apply_ln, k_col=k_col, v_col=v_col),
        name="inproj",
        grid=(M // tm, N // tn),
        in_specs=in_specs, out_specs=out_specs, out_shape=out_shape,
        input_output_aliases={n_in: 1, n_in + 1: 2},
        scratch_shapes=[pltpu.VMEM((tm, D), BF16)],
        compiler_params=_cparams(("parallel", "arbitrary")),
    )(*args)


def _pool_kernel(u_ref, g_ref, buf_ref, pw_ref, sc_ref, a_ref, ext, *, tt, pos0, gw):
    t = pl.program_id(1)

    @pl.when(t == 0)
    def _():
        ext[0:POOL_HALO, :] = buf_ref[...]

    ext[POOL_HALO:POOL_HALO + tt, :] = u_ref[...].astype(F32)
    row = lax.broadcasted_iota(jnp.int32, (tt, gw), 0)
    pos = (pos0 + t * tt + row).astype(F32)
    for g, w in enumerate(POOL_WINDOWS):
        cols = slice(g * gw, (g + 1) * gw)
        cur = ext[POOL_HALO:POOL_HALO + tt, cols]
        acc = cur
        for i in range(1, w):
            acc = acc + ext[POOL_HALO - i:POOL_HALO - i + tt, cols]
        cnt = jnp.minimum(pos + 1.0, float(w))
        diff = (acc / cnt - cur).astype(BF16)
        mixed = jnp.dot(diff, pw_ref[g], preferred_element_type=F32)
        gate = g_ref[:, cols].astype(F32)
        a_ref[:, cols] = (mixed * sc_ref[:, cols] * _silu(gate)).astype(a_ref.dtype)
    ext[0:POOL_HALO, :] = ext[tt:tt + POOL_HALO, :]


def _pool(h, buf16, pw_bf, scale, *, B, T, tt, pos0, D):
    nt = T // tt
    gw = D // POOL_GROUPS
    return pl.pallas_call(
        functools.partial(_pool_kernel, tt=tt, pos0=pos0, gw=gw),
        name="pool",
        grid=(B, nt),
        in_specs=[pl.BlockSpec((tt, D), lambda b, t: (b * nt + t, 0)),
                  pl.BlockSpec((tt, D), lambda b, t: (b * nt + t, 1)),
                  pl.BlockSpec((None, POOL_HALO, D), lambda b, t: (b, 0, 0)),
                  pl.BlockSpec((POOL_GROUPS, gw, gw), lambda b, t: (0, 0, 0)),
                  pl.BlockSpec((1, D), lambda b, t: (0, 0))],
        out_specs=pl.BlockSpec((tt, D), lambda b, t: (b * nt + t, 0)),
        out_shape=jax.ShapeDtypeStruct((B * T, D), h.dtype),
        scratch_shapes=[pltpu.VMEM((tt + POOL_HALO, D), F32)],
        compiler_params=_cparams(("parallel", "arbitrary")),
    )(h, h, buf16, pw_bf, scale.reshape(1, D))


def _ret_kernel(*refs, has_state, dk, dv):
    if has_state:
        (gc_ref, q_ref, k_ref, v_ref, g_ref, cos_ref, sin_ref, dm_ref, qd_ref, kd_ref, s0_ref,
         b_ref, sout_ref, s_scr) = refs
    else:
        (gc_ref, q_ref, k_ref, v_ref, g_ref, cos_ref, sin_ref, dm_ref, qd_ref, kd_ref,
         b_ref, sout_ref, s_scr) = refs
    n = pl.program_id(1)

    @pl.when(n == 0)
    def _():
        if has_state:
            s_scr[...] = s0_ref[...]
        else:
            s_scr[...] = jnp.zeros_like(s_scr)

    cos_t = cos_ref[...]
    sin_t = sin_ref[...]
    nt = (((1,), (1,)), ((), ()))
    tn = (((0,), (0,)), ((), ()))
    for hd in range(RET_HEADS):
        qk = slice(hd * dk, (hd + 1) * dk)
        vv = slice(hd * dv, (hd + 1) * dv)
        qh = q_ref[:, qk].astype(F32)
        kh = k_ref[:, qk].astype(F32)
        qh = qh * cos_t + pltpu.roll(qh, dk // 2, 1) * sin_t
        kh = (kh * cos_t + pltpu.roll(kh, dk // 2, 1) * sin_t) * (dk ** -0.5)
        vb = v_ref[:, vv].astype(BF16)
        scores = lax.dot_general(qh.astype(BF16), kh.astype(BF16), nt, preferred_element_type=F32) * dm_ref[hd]
        s_h = s_scr[hd]
        o = (jnp.dot(scores.astype(BF16), vb, preferred_element_type=F32)
             + jnp.dot((qh * qd_ref[hd]).astype(BF16), s_h.astype(BF16), preferred_element_type=F32))
        s_scr[hd] = gc_ref[hd] * s_h + lax.dot_general((kh * kd_ref[hd]).astype(BF16), vb, tn,
                                                       preferred_element_type=F32)
        mu = jnp.mean(o, axis=-1, keepdims=True)
        oc = o - mu
        var = jnp.mean(oc * oc, axis=-1, keepdims=True)
        gate = g_ref[:, vv].astype(F32)
        b_ref[:, vv] = (oc * lax.rsqrt(var + GN_EPS) * _silu(gate)).astype(b_ref.dtype)

    @pl.when(n == pl.num_programs(1) - 1)
    def _():
        sout_ref[...] = s_scr[...]


def _ret_tables(pos, chunk, dk):
    half = dk // 2
    freqs = RET_ROPE_BASE ** (-jnp.arange(half, dtype=F32) / half)
    ang = pos[:, None] * freqs[None, :]
    cos, sin = jnp.cos(ang), jnp.sin(ang)
    cos_t = jnp.concatenate([cos, cos], axis=-1)
    sin_t = jnp.concatenate([-sin, sin], axis=-1)
    log_g = jnp.log1p(-(2.0 ** (-5.0 - jnp.arange(RET_HEADS, dtype=F32))))
    idx = jnp.arange(chunk, dtype=F32)
    dist = idx[:, None] - idx[None, :]
    dmask = jnp.where(dist >= 0, jnp.exp(log_g[:, None, None] * jnp.maximum(dist, 0.0)), 0.0)
    q_dec = jnp.exp(log_g[:, None] * (idx[None, :] + 1.0))
    k_dec = jnp.exp(log_g[:, None] * (chunk - 1.0 - idx[None, :]))
    q_dec = jnp.broadcast_to(q_dec[:, :, None], (RET_HEADS, chunk, dk))
    k_dec = jnp.broadcast_to(k_dec[:, :, None], (RET_HEADS, chunk, dk))
    g_chunk = jnp.exp(log_g * chunk)
    return cos_t, sin_t, dmask, q_dec, k_dec, g_chunk


def _retention(h, state, pos, *, B, T, D):
    dk = D // RET_HEADS
    dv = 2 * dk
    C = RET_CHUNK if T % RET_CHUNK == 0 else T
    nc = T // C
    cos_t, sin_t, dmask, q_dec, k_dec, g_chunk = _ret_tables(pos, C, dk)
    has_state = state is not None
    blk = lambda w, c: pl.BlockSpec((C, w), lambda b, n: (b * nc + n, c))
    full3 = lambda a: pl.BlockSpec(a.shape, lambda b, n: (0, 0, 0))
    in_specs = [pl.BlockSpec(memory_space=pltpu.SMEM),
                blk(D, 2), blk(D, 3), blk(2 * D, 2), blk(2 * D, 3),
                pl.BlockSpec((C, dk), lambda b, n: (n, 0)), pl.BlockSpec((C, dk), lambda b, n: (n, 0)),
                full3(dmask), full3(q_dec), full3(k_dec)]
    args = [g_chunk, h, h, h, h, cos_t, sin_t, dmask, q_dec, k_dec]
    st_spec = pl.BlockSpec((None, RET_HEADS, dk, dv), lambda b, n: (b, 0, 0, 0))
    if has_state:
        in_specs.append(st_spec)
        args.append(state)
    return pl.pallas_call(
        functools.partial(_ret_kernel, has_state=has_state, dk=dk, dv=dv),
        name="retention",
        grid=(B, nc),
        in_specs=in_specs,
        out_specs=[pl.BlockSpec((C, 2 * D), lambda b, n: (b * nc + n, 0)), st_spec],
        out_shape=[jax.ShapeDtypeStruct((B * T, 2 * D), h.dtype),
                   jax.ShapeDtypeStruct((B, RET_HEADS, dk, dv), F32)],
        scratch_shapes=[pltpu.VMEM((RET_HEADS, dk, dv), F32)],
        compiler_params=_cparams(("parallel", "arbitrary")),
    )(*args)


MASKED = -1e30
LOG2E = 1.4426950408889634


def _exp(x):
    return jnp.exp2(x * LOG2E)


def _softplus(z):
    return jnp.maximum(z, 0.0) + jnp.log(1.0 + jnp.exp2(jnp.abs(z) * -LOG2E))


def _sb_prompt_kernel(tab_ref, bias_ref, q_ref, k_ref, v_ref, g_ref, c_ref,
                      z_scr, e0_scr, spb_scr, suf_scr, a_scr, acc_scr, carry_scr, *, n_blocks, n_chunks, R, dh):
    hp = pl.program_id(1)
    scale = dh ** -0.5
    b0 = bias_ref[2 * hp]
    b1 = bias_ref[2 * hp + 1]
    first = lax.broadcasted_iota(jnp.int32, (R, 2 * dh), 1) < dh
    r_i = lax.broadcasted_iota(jnp.int32, (R, R), 0)
    c_i = lax.broadcasted_iota(jnp.int32, (R, R), 1)
    later = jnp.where(r_i > c_i, 1.0, 0.0).astype(BF16)
    nt = (((1,), (1,)), ((), ()))

    def rows(b):
        return pl.multiple_of(tab_ref[0, b] * R, R)

    def keys(b):
        return pl.multiple_of(tab_ref[1, b] * R, R)

    def scores_stage(b, s):
        q = q_ref[pl.ds(rows(b), R), :].astype(F32) * scale
        qs = jnp.concatenate([jnp.where(first, q, 0.0), jnp.where(first, 0.0, q)], axis=0).astype(BF16)
        z_scr[s] = lax.dot_general(qs, k_ref[pl.ds(keys(b), R), :], nt, preferred_element_type=F32)

    def log_stage(b, s):
        i, j = tab_ref[0, b], tab_ref[1, b]
        z = z_scr[s]
        visible = (c_i - r_i) < (i - j) * R
        z = jnp.concatenate([jnp.where(visible, z[:R] + b0, MASKED), jnp.where(visible, z[R:] + b1, MASKED)],
                            axis=0)
        sp = _softplus(z)
        before = jnp.where(j == i, 0.0, carry_scr[...])
        e0_scr[s] = z - sp - before
        spb_scr[s] = sp.astype(BF16)
        carry_scr[...] = before + jnp.sum(sp, axis=1, keepdims=True)

    def suffix_stage(s):
        suf_scr[s] = jnp.dot(spb_scr[s], later, preferred_element_type=F32)

    def weight_stage(s):
        a_scr[s] = _exp(e0_scr[s] - suf_scr[s]).astype(BF16)

    def value_stage(b, s):
        acc_scr[tab_ref[0, b]] += jnp.dot(a_scr[s], v_ref[pl.ds(keys(b), R), :], preferred_element_type=F32)

    def step(h, p, valid):
        if valid(h - 2):
            suffix_stage(p)
        if valid(h - 4):
            value_stage(h - 4, p)
        if valid(h - 3):
            weight_stage(1 - p)
        if valid(h - 1):
            log_stage(h - 1, 1 - p)
        if valid(h):
            scores_stage(h, p)

    depth = 4
    in_range = lambda b: 0 <= b < n_blocks
    always = lambda b: True
    acc_scr[...] = jnp.zeros_like(acc_scr)
    carry_scr[...] = jnp.zeros_like(carry_scr)
    for h in range(min(depth, n_blocks + depth)):
        step(h, h % 2, in_range)
    unroll = 2
    n_iters = max(n_blocks - depth, 0) // unroll

    def steady_steps(t, carry):
        h = depth + unroll * t
        for k in range(unroll):
            step(h + k, (depth + k) % 2, always)
        return carry

    lax.fori_loop(0, n_iters, steady_steps, 0)
    for h in range(depth + unroll * n_iters, n_blocks + depth):
        step(h, h % 2, in_range)

    def write_chunk(i, carry):
        r0 = pl.multiple_of(i * R, R)
        acc = acc_scr[i]
        out = jnp.where(first, acc[:R], acc[R:])
        gate = g_ref[pl.ds(r0, R), :].astype(F32)
        c_ref[pl.ds(r0, R), :] = (out * _silu(gate)).astype(c_ref.dtype)
        return carry

    lax.fori_loop(0, n_chunks, write_chunk, 0)


def _sb_prompt(h, sb_bias, *, B, T, D, R):
    dh = D // SB_HEADS
    lanes = 2 * dh
    npair = SB_HEADS // 2
    n_chunks = T // R
    order = [(i, j) for i in range(n_chunks) for j in range(i, -1, -1)]
    table = jnp.asarray(np.array(order, np.int32).T)
    qc, kc, vc, gc = (off // lanes for off in (8 * D, 9 * D, 10 * D, 11 * D))
    blk = lambda c0: pl.BlockSpec((T, lanes), lambda b, p: (b, c0 + p))
    smem = pl.BlockSpec(memory_space=pltpu.SMEM)
    return pl.pallas_call(
        functools.partial(_sb_prompt_kernel, n_blocks=len(order), n_chunks=n_chunks, R=R, dh=dh),
        name="sb_prompt",
        grid=(B, npair),
        in_specs=[smem, smem, blk(qc), blk(kc), blk(vc), blk(gc)],
        out_specs=pl.BlockSpec((T, lanes), lambda b, p: (b, p)),
        out_shape=jax.ShapeDtypeStruct((B * T, D), BF16),
        scratch_shapes=[pltpu.VMEM((2, 2 * R, R), F32), pltpu.VMEM((2, 2 * R, R), F32),
                        pltpu.VMEM((2, 2 * R, R), BF16), pltpu.VMEM((2, 2 * R, R), F32),
                        pltpu.VMEM((2, 2 * R, R), BF16), pltpu.VMEM((n_chunks, 2 * R, lanes), F32),
                        pltpu.VMEM((2 * R, 1), F32)],
        compiler_params=_cparams(("parallel", "parallel")),
    )(table, sb_bias, h, h, h, h)


def _sb_decode_kernel(pt_ref, q_ref, knt_ref, vnt_ref, g_ref, bias_ref, later_ref, *rest, G, sub, ts, dh):
    k_refs = rest[:G]
    v_refs = rest[G:2 * G]
    c_ref, qbd, acc_t, carry = rest[2 * G:]
    p = pl.program_id(1)
    nq = SB_HEADS * ts
    width = SB_HEADS * dh
    scale = dh ** -0.5
    nt = (((1,), (1,)), ((), ()))
    bias = bias_ref[...]

    @pl.when(p == 0)
    def _():
        q = q_ref[...].astype(F32) * scale
        qt = jnp.concatenate([q] * SB_HEADS, axis=0)
        r_h = lax.broadcasted_iota(jnp.int32, (nq, width), 0) >> _log2(ts)
        c_h = lax.broadcasted_iota(jnp.int32, (nq, width), 1) >> _log2(dh)
        qbd[...] = jnp.where(r_h == c_h, qt, 0.0)
        z = jnp.dot(qbd[...], knt_ref[...].astype(F32), preferred_element_type=F32) + bias
        q_i = lax.broadcasted_iota(jnp.int32, (nq, ts), 0) & (ts - 1)
        t_i = lax.broadcasted_iota(jnp.int32, (nq, ts), 1)
        z = jnp.where(t_i < q_i, z, MASKED)
        sp = _softplus(z)
        suffix = jnp.zeros((nq, ts), F32)
        for t in range(1, ts):
            suffix = suffix + jnp.where(t_i < t, sp[:, t:t + 1], 0.0)
        a = _exp(z - sp - suffix)
        acc_t[...] = lax.dot_general(vnt_ref[...].astype(F32), a, nt, preferred_element_type=F32)
        carry[...] = jnp.sum(sp, axis=1, keepdims=True)

    later = later_ref[...]
    for s in range(G // sub):
        pages = [s * sub + g for g in reversed(range(sub))]
        z = jnp.dot(qbd[...], jnp.concatenate([k_refs[g][...] for g in pages], axis=1),
                    preferred_element_type=F32) + bias
        sp = _softplus(z)
        suffix = jnp.dot(sp.astype(BF16), later, preferred_element_type=F32)
        a = _exp(z - sp - suffix - carry[...])
        acc_t[...] += lax.dot_general(jnp.concatenate([v_refs[g][...] for g in pages], axis=1), a, nt,
                                      preferred_element_type=F32)
        carry[...] += jnp.sum(sp, axis=1, keepdims=True)

    @pl.when(p == pl.num_programs(1) - 1)
    def _():
        r_h = lax.broadcasted_iota(jnp.int32, (width, nq), 0) >> _log2(dh)
        c_h = lax.broadcasted_iota(jnp.int32, (width, nq), 1) >> _log2(ts)
        own = jnp.transpose(jnp.where(r_h == c_h, acc_t[...], 0.0))
        out = own[0:ts]
        for hd in range(1, SB_HEADS):
            out = out + own[hd * ts:(hd + 1) * ts]
        gate = g_ref[...].astype(F32)
        c_ref[...] = (out * _silu(gate)).astype(c_ref.dtype)


def _sb_decode(h, cache_k, cache_v, layer, page_table, sb_bias, *, B, ts, D, G, sub):
    dh = D // SB_HEADS
    n_pages = page_table.shape[1]
    page = cache_k.shape[3]
    nq = SB_HEADS * ts
    gp = sub * page
    steps = n_pages // G
    pt_flat = page_table.reshape(-1)
    bias_col = jnp.repeat(sb_bias, ts).reshape(nq, 1)
    later = jnp.tril(jnp.ones((gp, gp), BF16), k=-1)
    h3 = h.reshape(B, ts, -1)
    knt = jnp.swapaxes(h3[:, :, 9 * D:10 * D], 1, 2)
    vnt = jnp.swapaxes(h3[:, :, 10 * D:11 * D], 1, 2)
    hblk = lambda c: pl.BlockSpec((ts, D), lambda b, p, pt: (b, c))
    newblk = pl.BlockSpec((None, D, ts), lambda b, p, pt: (b, 0, 0))

    def page_spec(g):
        return pl.BlockSpec((None, None, D, page),
                            lambda b, p, pt: (layer, pt[b * n_pages + n_pages - 1 - (p * G + g)], 0, 0))

    grid_spec = pltpu.PrefetchScalarGridSpec(
        num_scalar_prefetch=1,
        grid=(B, steps),
        in_specs=[hblk(8), newblk, newblk, hblk(11),
                  pl.BlockSpec((nq, 1), lambda b, p, pt: (0, 0)),
                  pl.BlockSpec((gp, gp), lambda b, p, pt: (0, 0))]
                 + [page_spec(g) for g in range(G)] + [page_spec(g) for g in range(G)],
        out_specs=pl.BlockSpec((ts, D), lambda b, p, pt: (b, 0)),
        scratch_shapes=[pltpu.VMEM((nq, D), F32), pltpu.VMEM((D, nq), F32), pltpu.VMEM((nq, 1), F32)],
    )
    return pl.pallas_call(
        functools.partial(_sb_decode_kernel, G=G, sub=sub, ts=ts, dh=dh),
        name="sb_decode",
        grid_spec=grid_spec,
        out_shape=jax.ShapeDtypeStruct((B * ts, D), h.dtype),
        compiler_params=_cparams(("parallel", "arbitrary")),
    )(pt_flat, h, knt, vnt, h, bias_col, later, *([cache_k] * G), *([cache_v] * G))


def _out_kernel(a_ref, b_ref, c_ref, mp_ref, mr_ref, ms_ref, x_ref, pp_ref, pr_ref, ps_ref, wo_ref,
                lg_ref, lb_ref, y_ref, *, alpha):
    def branch(act_ref, proj_ref, gate_ref):
        return jax.nn.sigmoid(gate_ref[...].astype(F32)) * jnp.dot(act_ref[...].astype(BF16), proj_ref[...],
                                                                  preferred_element_type=F32)

    merged = branch(a_ref, pp_ref, mp_ref) + branch(b_ref, pr_ref, mr_ref) + branch(c_ref, ps_ref, ms_ref)
    y = alpha * x_ref[...] + jnp.dot(merged.astype(BF16), wo_ref[...], preferred_element_type=F32)
    y_ref[...] = _layer_norm_rows(y, lg_ref[...], lb_ref[...])


def _out_stage(a, b, c, h, x, pp, pr, ps, wo, lg, lb, *, alpha, tm):
    M, D = x.shape
    row = lambda w, cb: pl.BlockSpec((tm, w), lambda i: (i, cb))
    full = lambda arr: pl.BlockSpec(arr.shape, lambda i: (0, 0))
    vec = pl.BlockSpec((1, D), lambda i: (0, 0))
    return pl.pallas_call(
        functools.partial(_out_kernel, alpha=alpha),
        name="out_stage",
        grid=(M // tm,),
        in_specs=[row(D, 0), row(2 * D, 0), row(D, 0), row(D, 12), row(D, 13), row(D, 14), row(D, 0),
                  full(pp), full(pr), full(ps), full(wo), vec, vec],
        out_specs=row(D, 0),
        out_shape=jax.ShapeDtypeStruct((M, D), F32),
        compiler_params=_cparams(("parallel",)),
    )(a, b, c, h, h, h, x, pp, pr, ps, wo, lg.reshape(1, D), lb.reshape(1, D))


def _row_tile(m, cap):
    t = min(m, cap)
    while m % t:
        t //= 2
    return t


def kernel(x_prompt, x_sample, state_pool, state_ret, cache_sb_k, cache_sb_v, page_table, ln_in_g, ln_in_b,
           w_in, pool_w, pool_scale, proj_pool, proj_ret, proj_sb, w_out, ln_g, ln_b, sb_bias):
    Bp, Tp, D = x_prompt.shape
    Bs, Ts, _ = x_sample.shape
    depth = w_in.shape[0]
    n_pages = page_table.shape[1]
    page = cache_sb_k.shape[2]
    past_len = n_pages * page
    alpha = float((2 * depth) ** 0.25)
    assert SB_HEADS * Ts == 128 and D == 1024 and Tp % 256 == 0

    pos_p = jnp.arange(Tp, dtype=F32)
    pos_s = past_len + jnp.arange(Ts, dtype=F32)
    cache_k = jnp.transpose(cache_sb_k, (0, 1, 3, 4, 2)).reshape(depth, -1, D, page)
    cache_v = jnp.transpose(cache_sb_v, (0, 1, 3, 4, 2)).reshape(depth, -1, D, page)
    sub = 4 if n_pages % 4 == 0 else 1
    G = 2 * sub if n_pages % (2 * sub) == 0 else sub

    w_in_bf = w_in.astype(BF16)
    pool_w_bf = pool_w.astype(BF16)
    pp_bf, pr_bf, ps_bf, wo_bf = (w.astype(BF16) for w in (proj_pool, proj_ret, proj_sb, w_out))

    xp = x_prompt.reshape(Bp * Tp, D)
    xs = x_sample.reshape(Bs * Ts, D)
    tm_p = _row_tile(Tp, 1024)
    tm_s = _row_tile(Bs * Ts, 1024)
    zero_buf = jnp.zeros((Bp, POOL_HALO, D), F32)
    halo_pad = jnp.zeros((depth, Bs, POOL_HALO - state_pool.shape[2], D), F32)
    sample_buf = jnp.concatenate([halo_pad, state_pool], axis=2)
    n_keep = state_pool.shape[2]

    outs = {name: [] for name in ("pool_p", "pool_s", "ret_p", "ret_s")}
    kp_all, vp_all = (jnp.zeros((depth, Bp, SB_HEADS, D // SB_HEADS, Tp), F32) for _ in range(2))
    ks_all, vs_all = (jnp.zeros((depth, Bs * Ts, D), F32) for _ in range(2))
    for l in range(depth):
        ln = (ln_in_g, ln_in_b) if l == 0 else None
        res = _inproj(xp, w_in_bf[l], ln, kp_all, vp_all, l, k_col=9, v_col=10, tm=tm_p, h_dtype=BF16)
        h, kp_all, vp_all = res[:3]
        if l == 0:
            xp = res[3]
        a = _pool(h, zero_buf, pool_w_bf[l], pool_scale[l], B=Bp, T=Tp, tt=_row_tile(Tp, 256), pos0=0, D=D)
        b, s_new = _retention(h, None, pos_p, B=Bp, T=Tp, D=D)
        c = _sb_prompt(h, sb_bias[l], B=Bp, T=Tp, D=D, R=256)
        u_tail = h.reshape(Bp, Tp, -1)[:, Tp - n_keep:, :D].astype(F32)
        xp = _out_stage(a, b, c, h, xp, pp_bf[l], pr_bf[l], ps_bf[l], wo_bf[l], ln_g[l], ln_b[l],
                        alpha=alpha, tm=_row_tile(Bp * Tp, 512))
        outs["pool_p"].append(u_tail); outs["ret_p"].append(s_new)
        res = _inproj(xs, w_in_bf[l], ln, ks_all, vs_all, l, k_col=9, v_col=10, tm=tm_s, h_dtype=F32)
        h, ks_all, vs_all = res[:3]
        if l == 0:
            xs = res[3]
        a = _pool(h, sample_buf[l], pool_w_bf[l], pool_scale[l], B=Bs, T=Ts, tt=Ts, pos0=past_len, D=D)
        b, s_new = _retention(h, state_ret[l], pos_s, B=Bs, T=Ts, D=D)
        c = _sb_decode(h, cache_k, cache_v, l, page_table, sb_bias[l], B=Bs, ts=Ts, D=D, G=G, sub=sub)
        u_new = h.reshape(Bs, Ts, -1)[:, :, :D].astype(F32)
        u_tail = jnp.concatenate([state_pool[l], u_new], axis=1)[:, -n_keep:]
        xs = _out_stage(a, b, c, h, xs, pp_bf[l], pr_bf[l], ps_bf[l], wo_bf[l], ln_g[l], ln_b[l],
                        alpha=alpha, tm=tm_s)
        outs["pool_s"].append(u_tail); outs["ret_s"].append(s_new)

    st = lambda name: jnp.stack(outs[name])
    heads = lambda kv, B, T: kv.reshape(depth, B, T, SB_HEADS, D // SB_HEADS)
    return (xp.reshape(Bp, Tp, D), xs.reshape(Bs, Ts, D), st("pool_p"), st("pool_s"), st("ret_p"), st("ret_s"),
            jnp.transpose(kp_all, (0, 1, 4, 2, 3)), jnp.transpose(vp_all, (0, 1, 4, 2, 3)),
            heads(ks_all, Bs, Ts), heads(vs_all, Bs, Ts))
```

```python
import functools

import jax
import jax.numpy as jnp
import numpy as np
from jax import lax
from jax.experimental import pallas as pl
from jax.experimental.pallas import tpu as pltpu

F32 = jnp.float32
BF16 = jnp.bfloat16

POOL_WINDOWS = (2, 4, 8, 16)
POOL_GROUPS = len(POOL_WINDOWS)
POOL_HALO = 16
RET_HEADS = 8
RET_CHUNK = 128
RET_ROPE_BASE = 10000.0
SB_HEADS = 16
LN_EPS = 1e-5
GN_EPS = 1e-6
VMEM_LIMIT = 56 * 1024 * 1024


def _cparams(sem):
    return pltpu.CompilerParams(dimension_semantics=sem, vmem_limit_bytes=VMEM_LIMIT)


def _silu(x):
    return x * jax.nn.sigmoid(x)


def _log2(n):
    assert n & (n - 1) == 0
    return n.bit_length() - 1


def _layer_norm_rows(x, g, b):
    mu = jnp.mean(x, axis=-1, keepdims=True)
    xc = x - mu
    var = jnp.mean(xc * xc, axis=-1, keepdims=True)
    return xc * lax.rsqrt(var + LN_EPS) * g + b


def _inproj_kernel(*refs, apply_ln, k_col, v_col):
    if apply_ln:
        x_ref, g_ref, b_ref, w_ref, _k_all_in, _v_all_in, h_ref, k_ref, v_ref, xn_ref, xb = refs
    else:
        x_ref, w_ref, _k_all_in, _v_all_in, h_ref, k_ref, v_ref, xb = refs
    j = pl.program_id(1)

    @pl.when(j == 0)
    def _():
        x = x_ref[...]
        if apply_ln:
            x = _layer_norm_rows(x, g_ref[...], b_ref[...])
            xn_ref[...] = x
        xb[...] = x.astype(BF16)

    acc = jnp.dot(xb[...], w_ref[...], preferred_element_type=F32)
    h_ref[...] = acc.astype(h_ref.dtype)

    def emit(ref):
        ref[...] = acc if ref.ndim == 2 else acc.T.reshape(ref.shape)

    @pl.when(j == k_col)
    def _():
        emit(k_ref)

    @pl.when(j == v_col)
    def _():
        emit(v_ref)


def _inproj(x, w_bf, ln, k_all, v_all, layer, *, k_col, v_col, tm, h_dtype):
    M, D = x.shape
    N = w_bf.shape[1]
    tn = D
    apply_ln = ln is not None
    row = pl.BlockSpec((tm, D), lambda i, j: (i, 0))
    in_specs = [row]
    args = [x]
    if apply_ln:
        vec = pl.BlockSpec((1, D), lambda i, j: (0, 0))
        in_specs += [vec, vec]
        args += [ln[0].reshape(1, D), ln[1].reshape(1, D)]
    in_specs.append(pl.BlockSpec((D, tn), lambda i, j: (0, j)))
    args.append(w_bf)
    n_in = len(args)
    in_specs += [pl.BlockSpec(memory_space=pl.ANY)] * 2
    args += [k_all, v_all]
    if k_all.ndim == 3:
        layer_row = pl.BlockSpec((None, tm, D), lambda i, j: (layer, i, 0))
    else:
        _, _, heads, dh, T = k_all.shape
        per_seq = T // tm
        layer_row = pl.BlockSpec((None, None, heads, dh, tm),
                                 lambda i, j: (layer, i // per_seq, 0, 0, i % per_seq))
    out_shape = [jax.ShapeDtypeStruct((M, N), h_dtype), jax.ShapeDtypeStruct(k_all.shape, F32),
                 jax.ShapeDtypeStruct(v_all.shape, F32)]
    out_specs = [pl.BlockSpec((tm, tn), lambda i, j: (i, j)), layer_row, layer_row]
    if apply_ln:
        out_shape.append(jax.ShapeDtypeStruct((M, D), F32))
        out_specs.append(row)
    return pl.pallas_call(
        functools.partial(_inproj_kernel, apply_ln=apply_ln, k_col=k_col, v_col=v_col),
        name="inproj",
        grid=(M // tm, N // tn),
        in_specs=in_specs, out_specs=out_specs, out_shape=out_shape,
        input_output_aliases={n_in: 1, n_in + 1: 2},
        scratch_shapes=[pltpu.VMEM((tm, D), BF16)],
        compiler_params=_cparams(("parallel", "arbitrary")),
    )(*args)


def _pool_kernel(u_ref, g_ref, buf_ref, pw_ref, sc_ref, a_ref, ext, *, tt, pos0, gw):
    t = pl.program_id(1)

    @pl.when(t == 0)
    def _():
        ext[0:POOL_HALO, :] = buf_ref[...]

    ext[POOL_HALO:POOL_HALO + tt, :] = u_ref[...].astype(F32)
    row = lax.broadcasted_iota(jnp.int32, (tt, gw), 0)
    pos = (pos0 + t * tt + row).astype(F32)
    for g, w in enumerate(POOL_WINDOWS):
        cols = slice(g * gw, (g + 1) * gw)
        cur = ext[POOL_HALO:POOL_HALO + tt, cols]
        acc = cur
        for i in range(1, w):
            acc = acc + ext[POOL_HALO - i:POOL_HALO - i + tt, cols]
        cnt = jnp.minimum(pos + 1.0, float(w))
        diff = (acc / cnt - cur).astype(BF16)
        mixed = jnp.dot(diff, pw_ref[g], preferred_element_type=F32)
        gate = g_ref[:, cols].astype(F32)
        a_ref[:, cols] = (mixed * sc_ref[:, cols] * _silu(gate)).astype(a_ref.dtype)
    ext[0:POOL_HALO, :] = ext[tt:tt + POOL_HALO, :]


def _pool(h, buf16, pw_bf, scale, *, B, T, tt, pos0, D):
    nt = T // tt
    gw = D // POOL_GROUPS
    return pl.pallas_call(
        functools.partial(_pool_kernel, tt=tt, pos0=pos0, gw=gw),
        name="pool",
        grid=(B, nt),
        in_specs=[pl.BlockSpec((tt, D), lambda b, t: (b * nt + t, 0)),
                  pl.BlockSpec((tt, D), lambda b, t: (b * nt + t, 1)),
                  pl.BlockSpec((None, POOL_HALO, D), lambda b, t: (b, 0, 0)),
                  pl.BlockSpec((POOL_GROUPS, gw, gw), lambda b, t: (0, 0, 0)),
                  pl.BlockSpec((1, D), lambda b, t: (0, 0))],
        out_specs=pl.BlockSpec((tt, D), lambda b, t: (b * nt + t, 0)),
        out_shape=jax.ShapeDtypeStruct((B * T, D), h.dtype),
        scratch_shapes=[pltpu.VMEM((tt + POOL_HALO, D), F32)],
        compiler_params=_cparams(("parallel", "arbitrary")),
    )(h, h, buf16, pw_bf, scale.reshape(1, D))


def _ret_kernel(*refs, has_state, dk, dv):
    if has_state:
        (gc_ref, q_ref, k_ref, v_ref, g_ref, cos_ref, sin_ref, dm_ref, qd_ref, kd_ref, s0_ref,
         b_ref, sout_ref, s_scr) = refs
    else:
        (gc_ref, q_ref, k_ref, v_ref, g_ref, cos_ref, sin_ref, dm_ref, qd_ref, kd_ref,
         b_ref, sout_ref, s_scr) = refs
    n = pl.program_id(1)

    @pl.when(n == 0)
    def _():
        if has_state:
            s_scr[...] = s0_ref[...]
        else:
            s_scr[...] = jnp.zeros_like(s_scr)

    cos_t = cos_ref[...]
    sin_t = sin_ref[...]
    nt = (((1,), (1,)), ((), ()))
    tn = (((0,), (0,)), ((), ()))
    for hd in range(RET_HEADS):
        qk = slice(hd * dk, (hd + 1) * dk)
        vv = slice(hd * dv, (hd + 1) * dv)
        qh = q_ref[:, qk].astype(F32)
        kh = k_ref[:, qk].astype(F32)
        qh = qh * cos_t + pltpu.roll(qh, dk // 2, 1) * sin_t
        kh = (kh * cos_t + pltpu.roll(kh, dk // 2, 1) * sin_t) * (dk ** -0.5)
        vb = v_ref[:, vv].astype(BF16)
        scores = lax.dot_general(qh.astype(BF16), kh.astype(BF16), nt, preferred_element_type=F32) * dm_ref[hd]
        s_h = s_scr[hd]
        o = (jnp.dot(scores.astype(BF16), vb, preferred_element_type=F32)
             + jnp.dot((qh * qd_ref[hd]).astype(BF16), s_h.astype(BF16), preferred_element_type=F32))
        s_scr[hd] = gc_ref[hd] * s_h + lax.dot_general((kh * kd_ref[hd]).astype(BF16), vb, tn,
                                                       preferred_element_type=F32)
        mu = jnp.mean(o, axis=-1, keepdims=True)
        oc = o - mu
        var = jnp.mean(oc * oc, axis=-1, keepdims=True)
        gate = g_ref[:, vv].astype(F32)
        b_ref[:, vv] = (oc * lax.rsqrt(var + GN_EPS) * _silu(gate)).astype(b_ref.dtype)

    @pl.when(n == pl.num_programs(1) - 1)
    def _():
        sout_ref[...] = s_scr[...]


def _ret_tables(pos, chunk, dk):
    half = dk // 2
    freqs = RET_ROPE_BASE ** (-jnp.arange(half, dtype=F32) / half)
    ang = pos[:, None] * freqs[None, :]
    cos, sin = jnp.cos(ang), jnp.sin(ang)
    cos_t = jnp.concatenate([cos, cos], axis=-1)
    sin_t = jnp.concatenate([-sin, sin], axis=-1)
    log_g = jnp.log1p(-(2.0 ** (-5.0 - jnp.arange(RET_HEADS, dtype=F32))))
    idx = jnp.arange(chunk, dtype=F32)
    dist = idx[:, None] - idx[None, :]
    dmask = jnp.where(dist >= 0, jnp.exp(log_g[:, None, None] * jnp.maximum(dist, 0.0)), 0.0)
    q_dec = jnp.exp(log_g[:, None] * (idx[None, :] + 1.0))
    k_dec = jnp.exp(log_g[:, None] * (chunk - 1.0 - idx[None, :]))
    q_dec = jnp.broadcast_to(q_dec[:, :, None], (RET_HEADS, chunk, dk))
    k_dec = jnp.broadcast_to(k_dec[:, :, None], (RET_HEADS, chunk, dk))
    g_chunk = jnp.exp(log_g * chunk)
    return cos_t, sin_t, dmask, q_dec, k_dec, g_chunk


def _retention(h, state, pos, *, B, T, D):
    dk = D // RET_HEADS
    dv = 2 * dk
    C = RET_CHUNK if T % RET_CHUNK == 0 else T
    nc = T // C
    cos_t, sin_t, dmask, q_dec, k_dec, g_chunk = _ret_tables(pos, C, dk)
    has_state = state is not None
    blk = lambda w, c: pl.BlockSpec((C, w), lambda b, n: (b * nc + n, c))
    full3 = lambda a: pl.BlockSpec(a.shape, lambda b, n: (0, 0, 0))
    in_specs = [pl.BlockSpec(memory_space=pltpu.SMEM),
                blk(D, 2), blk(D, 3), blk(2 * D, 2), blk(2 * D, 3),
                pl.BlockSpec((C, dk), lambda b, n: (n, 0)), pl.BlockSpec((C, dk), lambda b, n: (n, 0)),
                full3(dmask), full3(q_dec), full3(k_dec)]
    args = [g_chunk, h, h, h, h, cos_t, sin_t, dmask, q_dec, k_dec]
    st_spec = pl.BlockSpec((None, RET_HEADS, dk, dv), lambda b, n: (b, 0, 0, 0))
    if has_state:
        in_specs.append(st_spec)
        args.append(state)
    return pl.pallas_call(
        functools.partial(_ret_kernel, has_state=has_state, dk=dk, dv=dv),
        name="retention",
        grid=(B, nc),
        in_specs=in_specs,
        out_specs=[pl.BlockSpec((C, 2 * D), lambda b, n: (b * nc + n, 0)), st_spec],
        out_shape=[jax.ShapeDtypeStruct((B * T, 2 * D), h.dtype),
                   jax.ShapeDtypeStruct((B, RET_HEADS, dk, dv), F32)],
        scratch_shapes=[pltpu.VMEM((RET_HEADS, dk, dv), F32)],
        compiler_params=_cparams(("parallel", "arbitrary")),
    )(*args)


MASKED = -1e30
LOG2E = 1.4426950408889634
SLAB = 64


def _exp(x):
    return jnp.exp2(x * LOG2E)


def _softplus(z):
    return jnp.maximum(z, 0.0) + jnp.log(1.0 + jnp.exp2(jnp.abs(z) * -LOG2E))


def _sb_prompt_kernel(tab_ref, bias_ref, q_ref, k_ref, v_ref, g_ref, c_ref,
                      z_scr, e0_scr, spb_scr, suf_scr, a_scr, acc_scr, carry_scr, *, n_blocks, n_chunks, R, dh):
    hp = pl.program_id(1)
    scale = dh ** -0.5
    b0 = bias_ref[2 * hp]
    b1 = bias_ref[2 * hp + 1]
    first = lax.broadcasted_iota(jnp.int32, (R, 2 * dh), 1) < dh
    r_i = lax.broadcasted_iota(jnp.int32, (R, R), 0)
    c_i = lax.broadcasted_iota(jnp.int32, (R, R), 1)
    later = jnp.where(r_i > c_i, 1.0, 0.0).astype(BF16)
    nt = (((1,), (1,)), ((), ()))

    def rows(b):
        return pl.multiple_of(tab_ref[0, b] * R, R)

    def keys(b):
        return pl.multiple_of(tab_ref[1, b] * R, R)

    def scores_stage(b, s):
        q = q_ref[pl.ds(rows(b), R), :].astype(F32) * scale
        qs = jnp.concatenate([jnp.where(first, q, 0.0), jnp.where(first, 0.0, q)], axis=0).astype(BF16)
        z_scr[s] = lax.dot_general(qs, k_ref[pl.ds(keys(b), R), :], nt, preferred_element_type=F32)

    def log_stage(b, s):
        i, j = tab_ref[0, b], tab_ref[1, b]
        for r0 in range(0, 2 * R, SLAB):
            rr = slice(r0, r0 + SLAB)
            q0 = r0 % R
            c_s = lax.broadcasted_iota(jnp.int32, (SLAB, R), 1)
            r_s = lax.broadcasted_iota(jnp.int32, (SLAB, R), 0) + q0
            visible = (c_s - r_s) < (i - j) * R
            z = jnp.where(visible, z_scr[s, rr, :] + (b0 if r0 < R else b1), MASKED)
            sp = _softplus(z)
            before = jnp.where(j == i, 0.0, carry_scr[rr, :])
            e0_scr[s, rr, :] = z - sp - before
            spb_scr[s, rr, :] = sp.astype(BF16)
            carry_scr[rr, :] = before + jnp.sum(sp, axis=1, keepdims=True)

    def suffix_stage(s):
        suf_scr[s] = jnp.dot(spb_scr[s], later, preferred_element_type=F32)

    def weight_stage(s):
        for r0 in range(0, 2 * R, SLAB):
            rr = slice(r0, r0 + SLAB)
            a_scr[s, rr, :] = _exp(e0_scr[s, rr, :] - suf_scr[s, rr, :]).astype(BF16)

    def value_stage(b, s):
        acc_scr[tab_ref[0, b]] += jnp.dot(a_scr[s], v_ref[pl.ds(keys(b), R), :], preferred_element_type=F32)

    def step(h, p, valid):
        if valid(h - 2):
            suffix_stage(p)
        if valid(h - 4):
            value_stage(h - 4, p)
        if valid(h - 3):
            weight_stage(1 - p)
        if valid(h - 1):
            log_stage(h - 1, 1 - p)
        if valid(h):
            scores_stage(h, p)

    depth = 4
    in_range = lambda b: 0 <= b < n_blocks
    always = lambda b: True
    acc_scr[...] = jnp.zeros_like(acc_scr)
    carry_scr[...] = jnp.zeros_like(carry_scr)
    for h in range(min(depth, n_blocks + depth)):
        step(h, h % 2, in_range)
    unroll = 2
    n_iters = max(n_blocks - depth, 0) // unroll

    def steady_steps(t, carry):
        h = depth + unroll * t
        for k in range(unroll):
            step(h + k, (depth + k) % 2, always)
        return carry

    lax.fori_loop(0, n_iters, steady_steps, 0)
    for h in range(depth + unroll * n_iters, n_blocks + depth):
        step(h, h % 2, in_range)

    def write_chunk(i, carry):
        r0 = pl.multiple_of(i * R, R)
        acc = acc_scr[i]
        out = jnp.where(first, acc[:R], acc[R:])
        gate = g_ref[pl.ds(r0, R), :].astype(F32)
        c_ref[pl.ds(r0, R), :] = (out * _silu(gate)).astype(c_ref.dtype)
        return carry

    lax.fori_loop(0, n_chunks, write_chunk, 0)


def _sb_prompt(h, sb_bias, *, B, T, D, R):
    dh = D // SB_HEADS
    lanes = 2 * dh
    npair = SB_HEADS // 2
    n_chunks = T // R
    order = [(i, j) for i in range(n_chunks) for j in range(i, -1, -1)]
    table = jnp.asarray(np.array(order, np.int32).T)
    qc, kc, vc, gc = (off // lanes for off in (8 * D, 9 * D, 10 * D, 11 * D))
    blk = lambda c0: pl.BlockSpec((T, lanes), lambda b, p: (b, c0 + p))
    smem = pl.BlockSpec(memory_space=pltpu.SMEM)
    return pl.pallas_call(
        functools.partial(_sb_prompt_kernel, n_blocks=len(order), n_chunks=n_chunks, R=R, dh=dh),
        name="sb_prompt",
        grid=(B, npair),
        in_specs=[smem, smem, blk(qc), blk(kc), blk(vc), blk(gc)],
        out_specs=pl.BlockSpec((T, lanes), lambda b, p: (b, p)),
        out_shape=jax.ShapeDtypeStruct((B * T, D), BF16),
        scratch_shapes=[pltpu.VMEM((2, 2 * R, R), F32), pltpu.VMEM((2, 2 * R, R), F32),
                        pltpu.VMEM((2, 2 * R, R), BF16), pltpu.VMEM((2, 2 * R, R), F32),
                        pltpu.VMEM((2, 2 * R, R), BF16), pltpu.VMEM((n_chunks, 2 * R, lanes), F32),
                        pltpu.VMEM((2 * R, 1), F32)],
        compiler_params=_cparams(("parallel", "parallel")),
    )(table, sb_bias, h, h, h, h)


def _sb_decode_kernel(pt_ref, q_ref, knt_ref, vnt_ref, g_ref, bias_ref, later_ref, *rest, G, sub, ts, dh):
    k_refs = rest[:G]
    v_refs = rest[G:2 * G]
    c_ref, qbd, acc_t, carry = rest[2 * G:]
    p = pl.program_id(1)
    nq = SB_HEADS * ts
    width = SB_HEADS * dh
    scale = dh ** -0.5
    nt = (((1,), (1,)), ((), ()))
    bias = bias_ref[...]

    @pl.when(p == 0)
    def _():
        q = q_ref[...].astype(F32) * scale
        qt = jnp.concatenate([q] * SB_HEADS, axis=0)
        r_h = lax.broadcasted_iota(jnp.int32, (nq, width), 0) >> _log2(ts)
        c_h = lax.broadcasted_iota(jnp.int32, (nq, width), 1) >> _log2(dh)
        qbd[...] = jnp.where(r_h == c_h, qt, 0.0)
        z = jnp.dot(qbd[...], knt_ref[...].astype(F32), preferred_element_type=F32) + bias
        q_i = lax.broadcasted_iota(jnp.int32, (nq, ts), 0) & (ts - 1)
        t_i = lax.broadcasted_iota(jnp.int32, (nq, ts), 1)
        z = jnp.where(t_i < q_i, z, MASKED)
        sp = _softplus(z)
        suffix = jnp.zeros((nq, ts), F32)
        for t in range(1, ts):
            suffix = suffix + jnp.where(t_i < t, sp[:, t:t + 1], 0.0)
        a = _exp(z - sp - suffix)
        acc_t[...] = lax.dot_general(vnt_ref[...].astype(F32), a, nt, preferred_element_type=F32)
        carry[...] = jnp.sum(sp, axis=1, keepdims=True)

    later = later_ref[...]
    for s in range(G // sub):
        pages = [s * sub + g for g in reversed(range(sub))]
        z = jnp.dot(qbd[...], jnp.concatenate([k_refs[g][...] for g in pages], axis=1),
                    preferred_element_type=F32) + bias
        sp = _softplus(z)
        suffix = jnp.dot(sp.astype(BF16), later, preferred_element_type=F32)
        a = _exp(z - sp - suffix - carry[...])
        acc_t[...] += lax.dot_general(jnp.concatenate([v_refs[g][...] for g in pages], axis=1), a, nt,
                                      preferred_element_type=F32)
        carry[...] += jnp.sum(sp, axis=1, keepdims=True)

    @pl.when(p == pl.num_programs(1) - 1)
    def _():
        r_h = lax.broadcasted_iota(jnp.int32, (width, nq), 0) >> _log2(dh)
        c_h = lax.broadcasted_iota(jnp.int32, (width, nq), 1) >> _log2(ts)
        own = jnp.transpose(jnp.where(r_h == c_h, acc_t[...], 0.0))
        out = own[0:ts]
        for hd in range(1, SB_HEADS):
            out = out + own[hd * ts:(hd + 1) * ts]
        gate = g_ref[...].astype(F32)
        c_ref[...] = (out * _silu(gate)).astype(c_ref.dtype)


def _sb_decode(h, cache_k, cache_v, layer, page_table, sb_bias, *, B, ts, D, G, sub):
    dh = D // SB_HEADS
    n_pages = page_table.shape[1]
    page = cache_k.shape[3]
    nq = SB_HEADS * ts
    gp = sub * page
    steps = n_pages // G
    pt_flat = page_table.reshape(-1)
    bias_col = jnp.repeat(sb_bias, ts).reshape(nq, 1)
    later = jnp.tril(jnp.ones((gp, gp), BF16), k=-1)
    h3 = h.reshape(B, ts, -1)
    knt = jnp.swapaxes(h3[:, :, 9 * D:10 * D], 1, 2)
    vnt = jnp.swapaxes(h3[:, :, 10 * D:11 * D], 1, 2)
    hblk = lambda c: pl.BlockSpec((ts, D), lambda b, p, pt: (b, c))
    newblk = pl.BlockSpec((None, D, ts), lambda b, p, pt: (b, 0, 0))

    def page_spec(g):
        return pl.BlockSpec((None, None, D, page),
                            lambda b, p, pt: (layer, pt[b * n_pages + n_pages - 1 - (p * G + g)], 0, 0))

    grid_spec = pltpu.PrefetchScalarGridSpec(
        num_scalar_prefetch=1,
        grid=(B, steps),
        in_specs=[hblk(8), newblk, newblk, hblk(11),
                  pl.BlockSpec((nq, 1), lambda b, p, pt: (0, 0)),
                  pl.BlockSpec((gp, gp), lambda b, p, pt: (0, 0))]
                 + [page_spec(g) for g in range(G)] + [page_spec(g) for g in range(G)],
        out_specs=pl.BlockSpec((ts, D), lambda b, p, pt: (b, 0)),
        scratch_shapes=[pltpu.VMEM((nq, D), F32), pltpu.VMEM((D, nq), F32), pltpu.VMEM((nq, 1), F32)],
    )
    return pl.pallas_call(
        functools.partial(_sb_decode_kernel, G=G, sub=sub, ts=ts, dh=dh),
        name="sb_decode",
        grid_spec=grid_spec,
        out_shape=jax.ShapeDtypeStruct((B * ts, D), h.dtype),
        compiler_params=_cparams(("parallel", "arbitrary")),
    )(pt_flat, h, knt, vnt, h, bias_col, later, *([cache_k] * G), *([cache_v] * G))


def _out_kernel(a_ref, b_ref, c_ref, mp_ref, mr_ref, ms_ref, x_ref, pp_ref, pr_ref, ps_ref, wo_ref,
                lg_ref, lb_ref, y_ref, *, alpha):
    def branch(act_ref, proj_ref, gate_ref):
        return jax.nn.sigmoid(gate_ref[...].astype(F32)) * jnp.dot(act_ref[...].astype(BF16), proj_ref[...],
                                                                  preferred_element_type=F32)

    merged = branch(a_ref, pp_ref, mp_ref) + branch(b_ref, pr_ref, mr_ref) + branch(c_ref, ps_ref, ms_ref)
    y = alpha * x_ref[...] + jnp.dot(merged.astype(BF16), wo_ref[...], preferred_element_type=F32)
    y_ref[...] = _layer_norm_rows(y, lg_ref[...], lb_ref[...])


def _out_stage(a, b, c, h, x, pp, pr, ps, wo, lg, lb, *, alpha, tm):
    M, D = x.shape
    row = lambda w, cb: pl.BlockSpec((tm, w), lambda i: (i, cb))
    full = lambda arr: pl.BlockSpec(arr.shape, lambda i: (0, 0))
    vec = pl.BlockSpec((1, D), lambda i: (0, 0))
    return pl.pallas_call(
        functools.partial(_out_kernel, alpha=alpha),
        name="out_stage",
        grid=(M // tm,),
        in_specs=[row(D, 0), row(2 * D, 0), row(D, 0), row(D, 12), row(D, 13), row(D, 14), row(D, 0),
                  full(pp), full(pr), full(ps), full(wo), vec, vec],
        out_specs=row(D, 0),
        out_shape=jax.ShapeDtypeStruct((M, D), F32),
        compiler_params=_cparams(("parallel",)),
    )(a, b, c, h, h, h, x, pp, pr, ps, wo, lg.reshape(1, D), lb.reshape(1, D))


def _row_tile(m, cap):
    t = min(m, cap)
    while m % t:
        t //= 2
    return t


def kernel(x_prompt, x_sample, state_pool, state_ret, cache_sb_k, cache_sb_v, page_table, ln_in_g, ln_in_b,
           w_in, pool_w, pool_scale, proj_pool, proj_ret, proj_sb, w_out, ln_g, ln_b, sb_bias):
    Bp, Tp, D = x_prompt.shape
    Bs, Ts, _ = x_sample.shape
    depth = w_in.shape[0]
    n_pages = page_table.shape[1]
    page = cache_sb_k.shape[2]
    past_len = n_pages * page
    alpha = float((2 * depth) ** 0.25)
    assert SB_HEADS * Ts == 128 and D == 1024 and Tp % 256 == 0

    pos_p = jnp.arange(Tp, dtype=F32)
    pos_s = past_len + jnp.arange(Ts, dtype=F32)
    cache_k = jnp.transpose(cache_sb_k, (0, 1, 3, 4, 2)).reshape(depth, -1, D, page)
    cache_v = jnp.transpose(cache_sb_v, (0, 1, 3, 4, 2)).reshape(depth, -1, D, page)
    sub = 4 if n_pages % 4 == 0 else 1
    G = sub
    while G < 4 * sub and n_pages % (2 * G) == 0:
        G *= 2

    w_in_bf = w_in.astype(BF16)
    pool_w_bf = pool_w.astype(BF16)
    pp_bf, pr_bf, ps_bf, wo_bf = (w.astype(BF16) for w in (proj_pool, proj_ret, proj_sb, w_out))

    xp = x_prompt.reshape(Bp * Tp, D)
    xs = x_sample.reshape(Bs * Ts, D)
    tm_p = _row_tile(Tp, 1024)
    tm_s = _row_tile(Bs * Ts, 1024)
    zero_buf = jnp.zeros((Bp, POOL_HALO, D), F32)
    halo_pad = jnp.zeros((depth, Bs, POOL_HALO - state_pool.shape[2], D), F32)
    sample_buf = jnp.concatenate([halo_pad, state_pool], axis=2)
    n_keep = state_pool.shape[2]

    outs = {name: [] for name in ("pool_p", "pool_s", "ret_p", "ret_s")}
    kp_all, vp_all = (jnp.zeros((depth, Bp, SB_HEADS, D // SB_HEADS, Tp), F32) for _ in range(2))
    ks_all, vs_all = (jnp.zeros((depth, Bs * Ts, D), F32) for _ in range(2))
    for l in range(depth):
        ln = (ln_in_g, ln_in_b) if l == 0 else None
        res = _inproj(xp, w_in_bf[l], ln, kp_all, vp_all, l, k_col=9, v_col=10, tm=tm_p, h_dtype=BF16)
        h, kp_all, vp_all = res[:3]
        if l == 0:
            xp = res[3]
        a = _pool(h, zero_buf, pool_w_bf[l], pool_scale[l], B=Bp, T=Tp, tt=_row_tile(Tp, 256), pos0=0, D=D)
        b, s_new = _retention(h, None, pos_p, B=Bp, T=Tp, D=D)
        c = _sb_prompt(h, sb_bias[l], B=Bp, T=Tp, D=D, R=256)
        u_tail = h.reshape(Bp, Tp, -1)[:, Tp - n_keep:, :D].astype(F32)
        xp = _out_stage(a, b, c, h, xp, pp_bf[l], pr_bf[l], ps_bf[l], wo_bf[l], ln_g[l], ln_b[l],
                        alpha=alpha, tm=_row_tile(Bp * Tp, 512))
        outs["pool_p"].append(u_tail); outs["ret_p"].append(s_new)
        res = _inproj(xs, w_in_bf[l], ln, ks_all, vs_all, l, k_col=9, v_col=10, tm=tm_s, h_dtype=F32)
        h, ks_all, vs_all = res[:3]
        if l == 0:
            xs = res[3]
        a = _pool(h, sample_buf[l], pool_w_bf[l], pool_scale[l], B=Bs, T=Ts, tt=Ts, pos0=past_len, D=D)
        b, s_new = _retention(h, state_ret[l], pos_s, B=Bs, T=Ts, D=D)
        c = _sb_decode(h, cache_k, cache_v, l, page_table, sb_bias[l], B=Bs, ts=Ts, D=D, G=G, sub=sub)
        u_new = h.reshape(Bs, Ts, -1)[:, :, :D].astype(F32)
        u_tail = jnp.concatenate([state_pool[l], u_new], axis=1)[:, -n_keep:]
        xs = _out_stage(a, b, c, h, xs, pp_bf[l], pr_bf[l], ps_bf[l], wo_bf[l], ln_g[l], ln_b[l],
                        alpha=alpha, tm=tm_s)
        outs["pool_s"].append(u_tail); outs["ret_s"].append(s_new)

    st = lambda name: jnp.stack(outs[name])
    heads = lambda kv, B, T: kv.reshape(depth, B, T, SB_HEADS, D // SB_HEADS)
    return (xp.reshape(Bp, Tp, D), xs.reshape(Bs, Ts, D), st("pool_p"), st("pool_s"), st("ret_p"), st("ret_s"),
            jnp.transpose(kp_all, (0, 1, 4, 2, 3)), jnp.transpose(vp_all, (0, 1, 4, 2, 3)),
            heads(ks_all, Bs, Ts), heads(vs_all, Bs, Ts))
```

```python
import functools

import jax
import jax.numpy as jnp
import numpy as np
from jax import lax
from jax.experimental import pallas as pl
from jax.experimental.pallas import tpu as pltpu

F32 = jnp.float32
BF16 = jnp.bfloat16

POOL_WINDOWS = (2, 4, 8, 16)
POOL_GROUPS = len(POOL_WINDOWS)
POOL_HALO = 16
RET_HEADS = 8
RET_CHUNK = 128
RET_ROPE_BASE = 10000.0
SB_HEADS = 16
LN_EPS = 1e-5
GN_EPS = 1e-6
VMEM_LIMIT = 56 * 1024 * 1024


def _cparams(sem):
    return pltpu.CompilerParams(dimension_semantics=sem, vmem_limit_bytes=VMEM_LIMIT)


def _silu(x):
    return x * jax.nn.sigmoid(x)


def _log2(n):
    assert n & (n - 1) == 0
    return n.bit_length() - 1


def _layer_norm_rows(x, g, b):
    mu = jnp.mean(x, axis=-1, keepdims=True)
    xc = x - mu
    var = jnp.mean(xc * xc, axis=-1, keepdims=True)
    return xc * lax.rsqrt(var + LN_EPS) * g + b


def _inproj_kernel(*refs, apply_ln, k_col, v_col):
    if apply_ln:
        x_ref, g_ref, b_ref, w_ref, _k_all_in, _v_all_in, h_ref, k_ref, v_ref, xn_ref, xb = refs
    else:
        x_ref, w_ref, _k_all_in, _v_all_in, h_ref, k_ref, v_ref, xb = refs
    j = pl.program_id(1)

    @pl.when(j == 0)
    def _():
        x = x_ref[...]
        if apply_ln:
            x = _layer_norm_rows(x, g_ref[...], b_ref[...])
            xn_ref[...] = x
        xb[...] = x.astype(BF16)

    acc = jnp.dot(xb[...], w_ref[...], preferred_element_type=F32)
    h_ref[...] = acc.astype(h_ref.dtype)

    def emit(ref):
        ref[...] = acc if ref.ndim == 2 else acc.T.reshape(ref.shape)

    @pl.when(j == k_col)
    def _():
        emit(k_ref)

    @pl.when(j == v_col)
    def _():
        emit(v_ref)


def _inproj(x, w_bf, ln, k_all, v_all, layer, *, k_col, v_col, tm, h_dtype):
    M, D = x.shape
    N = w_bf.shape[1]
    tn = D
    apply_ln = ln is not None
    row = pl.BlockSpec((tm, D), lambda i, j: (i, 0))
    in_specs = [row]
    args = [x]
    if apply_ln:
        vec = pl.BlockSpec((1, D), lambda i, j: (0, 0))
        in_specs += [vec, vec]
        args += [ln[0].reshape(1, D), ln[1].reshape(1, D)]
    in_specs.append(pl.BlockSpec((D, tn), lambda i, j: (0, j)))
    args.append(w_bf)
    n_in = len(args)
    in_specs += [pl.BlockSpec(memory_space=pl.ANY)] * 2
    args += [k_all, v_all]
    if k_all.ndim == 3:
        layer_row = pl.BlockSpec((None, tm, D), lambda i, j: (layer, i, 0))
    else:
        _, _, heads, dh, T = k_all.shape
        per_seq = T // tm
        layer_row = pl.BlockSpec((None, None, heads, dh, tm),
                                 lambda i, j: (layer, i // per_seq, 0, 0, i % per_seq))
    out_shape = [jax.ShapeDtypeStruct((M, N), h_dtype), jax.ShapeDtypeStruct(k_all.shape, F32),
                 jax.ShapeDtypeStruct(v_all.shape, F32)]
    out_specs = [pl.BlockSpec((tm, tn), lambda i, j: (i, j)), layer_row, layer_row]
    if apply_ln:
        out_shape.append(jax.ShapeDtypeStruct((M, D), F32))
        out_specs.append(row)
    return pl.pallas_call(
        functools.partial(_inproj_kernel, apply_ln=apply_ln, k_col=k_col, v_col=v_col),
        name="inproj",
        grid=(M // tm, N // tn),
        in_specs=in_specs, out_specs=out_specs, out_shape=out_shape,
        input_output_aliases={n_in: 1, n_in + 1: 2},
        scratch_shapes=[pltpu.VMEM((tm, D), BF16)],
        compiler_params=_cparams(("parallel", "arbitrary")),
    )(*args)


def _pool_kernel(u_ref, g_ref, buf_ref, pw_ref, sc_ref, a_ref, ext, *, tt, pos0, gw):
    t = pl.program_id(1)

    @pl.when(t == 0)
    def _():
        ext[0:POOL_HALO, :] = buf_ref[...]

    ext[POOL_HALO:POOL_HALO + tt, :] = u_ref[...].astype(F32)
    row = lax.broadcasted_iota(jnp.int32, (tt, gw), 0)
    pos = (pos0 + t * tt + row).astype(F32)
    for g, w in enumerate(POOL_WINDOWS):
        cols = slice(g * gw, (g + 1) * gw)
        cur = ext[POOL_HALO:POOL_HALO + tt, cols]
        acc = cur
        for i in range(1, w):
            acc = acc + ext[POOL_HALO - i:POOL_HALO - i + tt, cols]
        cnt = jnp.minimum(pos + 1.0, float(w))
        diff = (acc / cnt - cur).astype(BF16)
        mixed = jnp.dot(diff, pw_ref[g], preferred_element_type=F32)
        gate = g_ref[:, cols].astype(F32)
        a_ref[:, cols] = (mixed * sc_ref[:, cols] * _silu(gate)).astype(a_ref.dtype)
    ext[0:POOL_HALO, :] = ext[tt:tt + POOL_HALO, :]


def _pool(h, buf16, pw_bf, scale, *, B, T, tt, pos0, D):
    nt = T // tt
    gw = D // POOL_GROUPS
    return pl.pallas_call(
        functools.partial(_pool_kernel, tt=tt, pos0=pos0, gw=gw),
        name="pool",
        grid=(B, nt),
        in_specs=[pl.BlockSpec((tt, D), lambda b, t: (b * nt + t, 0)),
                  pl.BlockSpec((tt, D), lambda b, t: (b * nt + t, 1)),
                  pl.BlockSpec((None, POOL_HALO, D), lambda b, t: (b, 0, 0)),
                  pl.BlockSpec((POOL_GROUPS, gw, gw), lambda b, t: (0, 0, 0)),
                  pl.BlockSpec((1, D), lambda b, t: (0, 0))],
        out_specs=pl.BlockSpec((tt, D), lambda b, t: (b * nt + t, 0)),
        out_shape=jax.ShapeDtypeStruct((B * T, D), h.dtype),
        scratch_shapes=[pltpu.VMEM((tt + POOL_HALO, D), F32)],
        compiler_params=_cparams(("parallel", "arbitrary")),
    )(h, h, buf16, pw_bf, scale.reshape(1, D))


def _ret_kernel(*refs, has_state, dk, dv):
    if has_state:
        (gc_ref, q_ref, k_ref, v_ref, g_ref, cos_ref, sin_ref, dm_ref, qd_ref, kd_ref, s0_ref,
         b_ref, sout_ref, s_scr) = refs
    else:
        (gc_ref, q_ref, k_ref, v_ref, g_ref, cos_ref, sin_ref, dm_ref, qd_ref, kd_ref,
         b_ref, sout_ref, s_scr) = refs
    n = pl.program_id(1)

    @pl.when(n == 0)
    def _():
        if has_state:
            s_scr[...] = s0_ref[...]
        else:
            s_scr[...] = jnp.zeros_like(s_scr)

    cos_t = cos_ref[...]
    sin_t = sin_ref[...]
    nt = (((1,), (1,)), ((), ()))
    tn = (((0,), (0,)), ((), ()))
    for hd in range(RET_HEADS):
        qk = slice(hd * dk, (hd + 1) * dk)
        vv = slice(hd * dv, (hd + 1) * dv)
        qh = q_ref[:, qk].astype(F32)
        kh = k_ref[:, qk].astype(F32)
        qh = qh * cos_t + pltpu.roll(qh, dk // 2, 1) * sin_t
        kh = (kh * cos_t + pltpu.roll(kh, dk // 2, 1) * sin_t) * (dk ** -0.5)
        vb = v_ref[:, vv].astype(BF16)
        scores = lax.dot_general(qh.astype(BF16), kh.astype(BF16), nt, preferred_element_type=F32) * dm_ref[hd]
        s_h = s_scr[hd]
        o = (jnp.dot(scores.astype(BF16), vb, preferred_element_type=F32)
             + jnp.dot((qh * qd_ref[hd]).astype(BF16), s_h.astype(BF16), preferred_element_type=F32))
        s_scr[hd] = gc_ref[hd] * s_h + lax.dot_general((kh * kd_ref[hd]).astype(BF16), vb, tn,
                                                       preferred_element_type=F32)
        mu = jnp.mean(o, axis=-1, keepdims=True)
        oc = o - mu
        var = jnp.mean(oc * oc, axis=-1, keepdims=True)
        gate = g_ref[:, vv].astype(F32)
        b_ref[:, vv] = (oc * lax.rsqrt(var + GN_EPS) * _silu(gate)).astype(b_ref.dtype)

    @pl.when(n == pl.num_programs(1) - 1)
    def _():
        sout_ref[...] = s_scr[...]


def _ret_tables(pos, chunk, dk):
    half = dk // 2
    freqs = RET_ROPE_BASE ** (-jnp.arange(half, dtype=F32) / half)
    ang = pos[:, None] * freqs[None, :]
    cos, sin = jnp.cos(ang), jnp.sin(ang)
    cos_t = jnp.concatenate([cos, cos], axis=-1)
    sin_t = jnp.concatenate([-sin, sin], axis=-1)
    log_g = jnp.log1p(-(2.0 ** (-5.0 - jnp.arange(RET_HEADS, dtype=F32))))
    idx = jnp.arange(chunk, dtype=F32)
    dist = idx[:, None] - idx[None, :]
    dmask = jnp.where(dist >= 0, jnp.exp(log_g[:, None, None] * jnp.maximum(dist, 0.0)), 0.0)
    q_dec = jnp.exp(log_g[:, None] * (idx[None, :] + 1.0))
    k_dec = jnp.exp(log_g[:, None] * (chunk - 1.0 - idx[None, :]))
    q_dec = jnp.broadcast_to(q_dec[:, :, None], (RET_HEADS, chunk, dk))
    k_dec = jnp.broadcast_to(k_dec[:, :, None], (RET_HEADS, chunk, dk))
    g_chunk = jnp.exp(log_g * chunk)
    return cos_t, sin_t, dmask, q_dec, k_dec, g_chunk


def _retention(h, state, pos, *, B, T, D):
    dk = D // RET_HEADS
    dv = 2 * dk
    C = RET_CHUNK if T % RET_CHUNK == 0 else T
    nc = T // C
    cos_t, sin_t, dmask, q_dec, k_dec, g_chunk = _ret_tables(pos, C, dk)
    has_state = state is not None
    blk = lambda w, c: pl.BlockSpec((C, w), lambda b, n: (b * nc + n, c))
    full3 = lambda a: pl.BlockSpec(a.shape, lambda b, n: (0, 0, 0))
    in_specs = [pl.BlockSpec(memory_space=pltpu.SMEM),
                blk(D, 2), blk(D, 3), blk(2 * D, 2), blk(2 * D, 3),
                pl.BlockSpec((C, dk), lambda b, n: (n, 0)), pl.BlockSpec((C, dk), lambda b, n: (n, 0)),
                full3(dmask), full3(q_dec), full3(k_dec)]
    args = [g_chunk, h, h, h, h, cos_t, sin_t, dmask, q_dec, k_dec]
    st_spec = pl.BlockSpec((None, RET_HEADS, dk, dv), lambda b, n: (b, 0, 0, 0))
    if has_state:
        in_specs.append(st_spec)
        args.append(state)
    return pl.pallas_call(
        functools.partial(_ret_kernel, has_state=has_state, dk=dk, dv=dv),
        name="retention",
        grid=(B, nc),
        in_specs=in_specs,
        out_specs=[pl.BlockSpec((C, 2 * D), lambda b, n: (b * nc + n, 0)), st_spec],
        out_shape=[jax.ShapeDtypeStruct((B * T, 2 * D), h.dtype),
                   jax.ShapeDtypeStruct((B, RET_HEADS, dk, dv), F32)],
        scratch_shapes=[pltpu.VMEM((RET_HEADS, dk, dv), F32)],
        compiler_params=_cparams(("parallel", "arbitrary")),
    )(*args)


MASKED = -1e30
LOG2E = 1.4426950408889634
SLAB = 128


def _exp(x):
    return jnp.exp2(x * LOG2E)


def _softplus(z):
    return jnp.maximum(z, 0.0) + jnp.log(1.0 + jnp.exp2(jnp.abs(z) * -LOG2E))


def _sb_prompt_kernel(tab_ref, bias_ref, q_ref, k_ref, v_ref, g_ref, c_ref,
                      z_scr, e0_scr, spb_scr, suf_scr, a_scr, acc_scr, carry_scr, *, n_blocks, n_chunks, R, dh):
    hp = pl.program_id(1)
    scale = dh ** -0.5
    b0 = bias_ref[2 * hp]
    b1 = bias_ref[2 * hp + 1]
    first = lax.broadcasted_iota(jnp.int32, (R, 2 * dh), 1) < dh
    r_i = lax.broadcasted_iota(jnp.int32, (R, R), 0)
    c_i = lax.broadcasted_iota(jnp.int32, (R, R), 1)
    later = jnp.where(r_i > c_i, 1.0, 0.0).astype(BF16)
    nt = (((1,), (1,)), ((), ()))

    def rows(b):
        return pl.multiple_of(tab_ref[0, b] * R, R)

    def keys(b):
        return pl.multiple_of(tab_ref[1, b] * R, R)

    def scores_stage(b, s):
        q = q_ref[pl.ds(rows(b), R), :].astype(F32) * scale
        qs = jnp.concatenate([jnp.where(first, q, 0.0), jnp.where(first, 0.0, q)], axis=0).astype(BF16)
        z_scr[s] = lax.dot_general(qs, k_ref[pl.ds(keys(b), R), :], nt, preferred_element_type=F32)

    def log_stage(b, s):
        i, j = tab_ref[0, b], tab_ref[1, b]
        for r0 in range(0, 2 * R, SLAB):
            rr = slice(r0, r0 + SLAB)
            q0 = r0 % R
            c_s = lax.broadcasted_iota(jnp.int32, (SLAB, R), 1)
            r_s = lax.broadcasted_iota(jnp.int32, (SLAB, R), 0) + q0
            visible = (c_s - r_s) < (i - j) * R
            z = jnp.where(visible, z_scr[s, rr, :] + (b0 if r0 < R else b1), MASKED)
            sp = _softplus(z)
            before = jnp.where(j == i, 0.0, carry_scr[rr, :])
            e0_scr[s, rr, :] = z - sp - before
            spb_scr[s, rr, :] = sp.astype(BF16)
            carry_scr[rr, :] = before + jnp.sum(sp, axis=1, keepdims=True)

    def suffix_stage(s):
        suf_scr[s] = jnp.dot(spb_scr[s], later, preferred_element_type=F32)

    def weight_stage(s):
        for r0 in range(0, 2 * R, SLAB):
            rr = slice(r0, r0 + SLAB)
            a_scr[s, rr, :] = _exp(e0_scr[s, rr, :] - suf_scr[s, rr, :]).astype(BF16)

    def value_stage(b, s):
        acc_scr[tab_ref[0, b]] += jnp.dot(a_scr[s], v_ref[pl.ds(keys(b), R), :], preferred_element_type=F32)

    def step(h, p, valid):
        if valid(h - 2):
            suffix_stage(p)
        if valid(h - 4):
            value_stage(h - 4, p)
        if valid(h - 3):
            weight_stage(1 - p)
        if valid(h - 1):
            log_stage(h - 1, 1 - p)
        if valid(h):
            scores_stage(h, p)

    depth = 4
    in_range = lambda b: 0 <= b < n_blocks
    always = lambda b: True
    acc_scr[...] = jnp.zeros_like(acc_scr)
    carry_scr[...] = jnp.zeros_like(carry_scr)
    for h in range(min(depth, n_blocks + depth)):
        step(h, h % 2, in_range)
    unroll = 2
    n_iters = max(n_blocks - depth, 0) // unroll

    def steady_steps(t, carry):
        h = depth + unroll * t
        for k in range(unroll):
            step(h + k, (depth + k) % 2, always)
        return carry

    lax.fori_loop(0, n_iters, steady_steps, 0)
    for h in range(depth + unroll * n_iters, n_blocks + depth):
        step(h, h % 2, in_range)

    def write_chunk(i, carry):
        r0 = pl.multiple_of(i * R, R)
        acc = acc_scr[i]
        out = jnp.where(first, acc[:R], acc[R:])
        gate = g_ref[pl.ds(r0, R), :].astype(F32)
        c_ref[pl.ds(r0, R), :] = (out * _silu(gate)).astype(c_ref.dtype)
        return carry

    lax.fori_loop(0, n_chunks, write_chunk, 0)


def _sb_prompt(h, sb_bias, *, B, T, D, R):
    dh = D // SB_HEADS
    lanes = 2 * dh
    npair = SB_HEADS // 2
    n_chunks = T // R
    order = [(i, j) for i in range(n_chunks) for j in range(i, -1, -1)]
    table = jnp.asarray(np.array(order, np.int32).T)
    qc, kc, vc, gc = (off // lanes for off in (8 * D, 9 * D, 10 * D, 11 * D))
    blk = lambda c0: pl.BlockSpec((T, lanes), lambda b, p: (b, c0 + p))
    smem = pl.BlockSpec(memory_space=pltpu.SMEM)
    return pl.pallas_call(
        functools.partial(_sb_prompt_kernel, n_blocks=len(order), n_chunks=n_chunks, R=R, dh=dh),
        name="sb_prompt",
        grid=(B, npair),
        in_specs=[smem, smem, blk(qc), blk(kc), blk(vc), blk(gc)],
        out_specs=pl.BlockSpec((T, lanes), lambda b, p: (b, p)),
        out_shape=jax.ShapeDtypeStruct((B * T, D), BF16),
        scratch_shapes=[pltpu.VMEM((2, 2 * R, R), F32), pltpu.VMEM((2, 2 * R, R), F32),
                        pltpu.VMEM((2, 2 * R, R), BF16), pltpu.VMEM((2, 2 * R, R), F32),
                        pltpu.VMEM((2, 2 * R, R), BF16), pltpu.VMEM((n_chunks, 2 * R, lanes), F32),
                        pltpu.VMEM((2 * R, 1), F32)],
        compiler_params=_cparams(("parallel", "parallel")),
    )(table, sb_bias, h, h, h, h)


def _sb_decode_kernel(pt_ref, q_ref, knt_ref, vnt_ref, g_ref, bias_ref, later_ref, *rest, G, sub, ts, dh):
    k_refs = rest[:G]
    v_refs = rest[G:2 * G]
    c_ref, qbd, acc_t, carry = rest[2 * G:]
    p = pl.program_id(1)
    nq = SB_HEADS * ts
    width = SB_HEADS * dh
    scale = dh ** -0.5
    nt = (((1,), (1,)), ((), ()))
    bias = bias_ref[...]

    @pl.when(p == 0)
    def _():
        q = q_ref[...].astype(F32) * scale
        qt = jnp.concatenate([q] * SB_HEADS, axis=0)
        r_h = lax.broadcasted_iota(jnp.int32, (nq, width), 0) >> _log2(ts)
        c_h = lax.broadcasted_iota(jnp.int32, (nq, width), 1) >> _log2(dh)
        qbd[...] = jnp.where(r_h == c_h, qt, 0.0)
        z = jnp.dot(qbd[...], knt_ref[...].astype(F32), preferred_element_type=F32) + bias
        q_i = lax.broadcasted_iota(jnp.int32, (nq, ts), 0) & (ts - 1)
        t_i = lax.broadcasted_iota(jnp.int32, (nq, ts), 1)
        z = jnp.where(t_i < q_i, z, MASKED)
        sp = _softplus(z)
        suffix = jnp.zeros((nq, ts), F32)
        for t in range(1, ts):
            suffix = suffix + jnp.where(t_i < t, sp[:, t:t + 1], 0.0)
        a = _exp(z - sp - suffix)
        acc_t[...] = lax.dot_general(vnt_ref[...].astype(F32), a, nt, preferred_element_type=F32)
        carry[...] = jnp.sum(sp, axis=1, keepdims=True)

    later = later_ref[...]
    for s in range(G // sub):
        pages = [s * sub + g for g in reversed(range(sub))]
        z = jnp.dot(qbd[...], jnp.concatenate([k_refs[g][...] for g in pages], axis=1),
                    preferred_element_type=F32) + bias
        sp = _softplus(z)
        suffix = jnp.dot(sp.astype(BF16), later, preferred_element_type=F32)
        a = _exp(z - sp - suffix - carry[...])
        acc_t[...] += lax.dot_general(jnp.concatenate([v_refs[g][...] for g in pages], axis=1), a, nt,
                                      preferred_element_type=F32)
        carry[...] += jnp.sum(sp, axis=1, keepdims=True)

    @pl.when(p == pl.num_programs(1) - 1)
    def _():
        r_h = lax.broadcasted_iota(jnp.int32, (width, nq), 0) >> _log2(dh)
        c_h = lax.broadcasted_iota(jnp.int32, (width, nq), 1) >> _log2(ts)
        own = jnp.transpose(jnp.where(r_h == c_h, acc_t[...], 0.0))
        out = own[0:ts]
        for hd in range(1, SB_HEADS):
            out = out + own[hd * ts:(hd + 1) * ts]
        gate = g_ref[...].astype(F32)
        c_ref[...] = (out * _silu(gate)).astype(c_ref.dtype)


def _sb_decode(h, cache_k, cache_v, layer, page_table, sb_bias, *, B, ts, D, G, sub):
    dh = D // SB_HEADS
    n_pages = page_table.shape[1]
    page = cache_k.shape[3]
    nq = SB_HEADS * ts
    gp = sub * page
    steps = n_pages // G
    pt_flat = page_table.reshape(-1)
    bias_col = jnp.repeat(sb_bias, ts).reshape(nq, 1)
    later = jnp.tril(jnp.ones((gp, gp), BF16), k=-1)
    h3 = h.reshape(B, ts, -1)
    knt = jnp.swapaxes(h3[:, :, 9 * D:10 * D], 1, 2)
    vnt = jnp.swapaxes(h3[:, :, 10 * D:11 * D], 1, 2)
    hblk = lambda c: pl.BlockSpec((ts, D), lambda b, p, pt: (b, c))
    newblk = pl.BlockSpec((None, D, ts), lambda b, p, pt: (b, 0, 0))

    def page_spec(g):
        return pl.BlockSpec((None, None, D, page),
                            lambda b, p, pt: (layer, pt[b * n_pages + n_pages - 1 - (p * G + g)], 0, 0))

    grid_spec = pltpu.PrefetchScalarGridSpec(
        num_scalar_prefetch=1,
        grid=(B, steps),
        in_specs=[hblk(8), newblk, newblk, hblk(11),
                  pl.BlockSpec((nq, 1), lambda b, p, pt: (0, 0)),
                  pl.BlockSpec((gp, gp), lambda b, p, pt: (0, 0))]
                 + [page_spec(g) for g in range(G)] + [page_spec(g) for g in range(G)],
        out_specs=pl.BlockSpec((ts, D), lambda b, p, pt: (b, 0)),
        scratch_shapes=[pltpu.VMEM((nq, D), F32), pltpu.VMEM((D, nq), F32), pltpu.VMEM((nq, 1), F32)],
    )
    return pl.pallas_call(
        functools.partial(_sb_decode_kernel, G=G, sub=sub, ts=ts, dh=dh),
        name="sb_decode",
        grid_spec=grid_spec,
        out_shape=jax.ShapeDtypeStruct((B * ts, D), h.dtype),
        compiler_params=_cparams(("parallel", "arbitrary")),
    )(pt_flat, h, knt, vnt, h, bias_col, later, *([cache_k] * G), *([cache_v] * G))


def _out_kernel(a_ref, b_ref, c_ref, mp_ref, mr_ref, ms_ref, x_ref, pp_ref, pr_ref, ps_ref, wo_ref,
                lg_ref, lb_ref, y_ref, *, alpha):
    def branch(act_ref, proj_ref, gate_ref):
        return jax.nn.sigmoid(gate_ref[...].astype(F32)) * jnp.dot(act_ref[...].astype(BF16), proj_ref[...],
                                                                  preferred_element_type=F32)

    merged = branch(a_ref, pp_ref, mp_ref) + branch(b_ref, pr_ref, mr_ref) + branch(c_ref, ps_ref, ms_ref)
    y = alpha * x_ref[...] + jnp.dot(merged.astype(BF16), wo_ref[...], preferred_element_type=F32)
    y_ref[...] = _layer_norm_rows(y, lg_ref[...], lb_ref[...])


def _out_stage(a, b, c, h, x, pp, pr, ps, wo, lg, lb, *, alpha, tm):
    M, D = x.shape
    row = lambda w, cb: pl.BlockSpec((tm, w), lambda i: (i, cb))
    full = lambda arr: pl.BlockSpec(arr.shape, lambda i: (0, 0))
    vec = pl.BlockSpec((1, D), lambda i: (0, 0))
    return pl.pallas_call(
        functools.partial(_out_kernel, alpha=alpha),
        name="out_stage",
        grid=(M // tm,),
        in_specs=[row(D, 0), row(2 * D, 0), row(D, 0), row(D, 12), row(D, 13), row(D, 14), row(D, 0),
                  full(pp), full(pr), full(ps), full(wo), vec, vec],
        out_specs=row(D, 0),
        out_shape=jax.ShapeDtypeStruct((M, D), F32),
        compiler_params=_cparams(("parallel",)),
    )(a, b, c, h, h, h, x, pp, pr, ps, wo, lg.reshape(1, D), lb.reshape(1, D))


def _row_tile(m, cap):
    t = min(m, cap)
    while m % t:
        t //= 2
    return t


def kernel(x_prompt, x_sample, state_pool, state_ret, cache_sb_k, cache_sb_v, page_table, ln_in_g, ln_in_b,
           w_in, pool_w, pool_scale, proj_pool, proj_ret, proj_sb, w_out, ln_g, ln_b, sb_bias):
    Bp, Tp, D = x_prompt.shape
    Bs, Ts, _ = x_sample.shape
    depth = w_in.shape[0]
    n_pages = page_table.shape[1]
    page = cache_sb_k.shape[2]
    past_len = n_pages * page
    alpha = float((2 * depth) ** 0.25)
    assert SB_HEADS * Ts == 128 and D == 1024 and Tp % 256 == 0

    pos_p = jnp.arange(Tp, dtype=F32)
    pos_s = past_len + jnp.arange(Ts, dtype=F32)
    cache_k = jnp.transpose(cache_sb_k, (0, 1, 3, 4, 2)).reshape(depth, -1, D, page)
    cache_v = jnp.transpose(cache_sb_v, (0, 1, 3, 4, 2)).reshape(depth, -1, D, page)
    sub = 4 if n_pages % 4 == 0 else 1
    G = sub
    while G < 4 * sub and n_pages % (2 * G) == 0:
        G *= 2

    w_in_bf = w_in.astype(BF16)
    pool_w_bf = pool_w.astype(BF16)
    pp_bf, pr_bf, ps_bf, wo_bf = (w.astype(BF16) for w in (proj_pool, proj_ret, proj_sb, w_out))

    xp = x_prompt.reshape(Bp * Tp, D)
    xs = x_sample.reshape(Bs * Ts, D)
    tm_p = _row_tile(Tp, 1024)
    tm_s = _row_tile(Bs * Ts, 1024)
    zero_buf = jnp.zeros((Bp, POOL_HALO, D), F32)
    halo_pad = jnp.zeros((depth, Bs, POOL_HALO - state_pool.shape[2], D), F32)
    sample_buf = jnp.concatenate([halo_pad, state_pool], axis=2)
    n_keep = state_pool.shape[2]

    outs = {name: [] for name in ("pool_p", "pool_s", "ret_p", "ret_s")}
    kp_all, vp_all = (jnp.zeros((depth, Bp, SB_HEADS, D // SB_HEADS, Tp), F32) for _ in range(2))
    ks_all, vs_all = (jnp.zeros((depth, Bs * Ts, D), F32) for _ in range(2))
    for l in range(depth):
        ln = (ln_in_g, ln_in_b) if l == 0 else None
        res = _inproj(xp, w_in_bf[l], ln, kp_all, vp_all, l, k_col=9, v_col=10, tm=tm_p, h_dtype=BF16)
        h, kp_all, vp_all = res[:3]
        if l == 0:
            xp = res[3]
        a = _pool(h, zero_buf, pool_w_bf[l], pool_scale[l], B=Bp, T=Tp, tt=_row_tile(Tp, 256), pos0=0, D=D)
        b, s_new = _retention(h, None, pos_p, B=Bp, T=Tp, D=D)
        c = _sb_prompt(h, sb_bias[l], B=Bp, T=Tp, D=D, R=256)
        u_tail = h.reshape(Bp, Tp, -1)[:, Tp - n_keep:, :D].astype(F32)
        xp = _out_stage(a, b, c, h, xp, pp_bf[l], pr_bf[l], ps_bf[l], wo_bf[l], ln_g[l], ln_b[l],
                        alpha=alpha, tm=_row_tile(Bp * Tp, 512))
        outs["pool_p"].append(u_tail); outs["ret_p"].append(s_new)
        res = _inproj(xs, w_in_bf[l], ln, ks_all, vs_all, l, k_col=9, v_col=10, tm=tm_s, h_dtype=F32)
        h, ks_all, vs_all = res[:3]
        if l == 0:
            xs = res[3]
        a = _pool(h, sample_buf[l], pool_w_bf[l], pool_scale[l], B=Bs, T=Ts, tt=Ts, pos0=past_len, D=D)
        b, s_new = _retention(h, state_ret[l], pos_s, B=Bs, T=Ts, D=D)
        c = _sb_decode(h, cache_k, cache_v, l, page_table, sb_bias[l], B=Bs, ts=Ts, D=D, G=G, sub=sub)
        u_new = h.reshape(Bs, Ts, -1)[:, :, :D].astype(F32)
        u_tail = jnp.concatenate([state_pool[l], u_new], axis=1)[:, -n_keep:]
        xs = _out_stage(a, b, c, h, xs, pp_bf[l], pr_bf[l], ps_bf[l], wo_bf[l], ln_g[l], ln_b[l],
                        alpha=alpha, tm=tm_s)
        outs["pool_s"].append(u_tail); outs["ret_s"].append(s_new)

    st = lambda name: jnp.stack(outs[name])
    heads = lambda kv, B, T: kv.reshape(depth, B, T, SB_HEADS, D // SB_HEADS)
    return (xp.reshape(Bp, Tp, D), xs.reshape(Bs, Ts, D), st("pool_p"), st("pool_s"), st("ret_p"), st("ret_s"),
            jnp.transpose(kp_all, (0, 1, 4, 2, 3)), jnp.transpose(vp_all, (0, 1, 4, 2, 3)),
            heads(ks_all, Bs, Ts), heads(vs_all, Bs, Ts))
```
